```python
import jax, jax.numpy as jnp
from jax import lax
import numpy as np

D_MODEL = 2048
BATCH = 4
SEQ = 2048
DEPTH = 1
DEC_BATCH = 32
DEC_SEQ = 4
PAST_LEN = 16384
PAGE_SIZE = 128

HEAD_DIM = 128
N_HEADS_FOX = D_MODEL // (2 * HEAD_DIM)
N_HEADS_SB = D_MODEL // (2 * HEAD_DIM)
WIDTH_FOX = N_HEADS_FOX * HEAD_DIM
WIDTH_SB = N_HEADS_SB * HEAD_DIM
BLOCK_Q = 128
RMS_EPS = 1e-6
FORGET_BIAS = 8.0
PROJ_SIZES = (WIDTH_FOX, WIDTH_FOX, WIDTH_FOX, WIDTH_FOX, WIDTH_SB, WIDTH_SB, WIDTH_SB, WIDTH_SB, N_HEADS_FOX, D_MODEL, D_MODEL)
PROJ_WIDTH = 4 * WIDTH_FOX + 4 * WIDTH_SB + N_HEADS_FOX + 2 * D_MODEL

kernel_name = "fox_stickbreak_gated_hybrid_step"


def rmsnorm(x, g):
    x32 = x.astype(jnp.float32)
    y = x32 * lax.rsqrt(jnp.mean(x32 * x32, axis=-1, keepdims=True) + RMS_EPS) * g.astype(jnp.float32)
    return y.astype(x.dtype)


def project_inputs(h, w_in, b_f):
    proj = jnp.einsum('btd,dc->btc', h, w_in)
    pts = []
    acc = 0
    for s in PROJ_SIZES[:-1]:
        acc += s
        pts.append(acc)
    q_a, k_a, v_a, g_a, q_b, k_b, v_b, g_b, f_a, r_a, r_b = jnp.split(proj, pts, axis=-1)
    heads = lambda t, n: t.reshape(t.shape[:-1] + (n, HEAD_DIM))
    logf = jax.nn.log_sigmoid(f_a.astype(jnp.float32) + b_f.astype(jnp.float32))
    fox = (heads(q_a, N_HEADS_FOX), heads(k_a, N_HEADS_FOX), heads(v_a, N_HEADS_FOX), logf)
    sb = (heads(q_b, N_HEADS_SB), heads(k_b, N_HEADS_SB), heads(v_b, N_HEADS_SB))
    gates = (g_a, g_b, r_a, r_b)
    return fox, sb, gates


def suffix_exclusive(logf, axis):
    return lax.cumsum(logf, axis=axis, reverse=True) - logf


def fox_attend(q, k, v, sfx_q, sfx_k, q_pos, k_pos):
    logits = jnp.einsum('bqhd,bkhd->bhqk', q.astype(jnp.float32), k.astype(jnp.float32)) * (HEAD_DIM ** -0.5)
    decay = jnp.transpose(sfx_k, (0, 2, 1))[:, :, None, :] - jnp.transpose(sfx_q, (0, 2, 1))[:, :, :, None]
    causal = k_pos[None, :] <= q_pos[:, None]
    logits = jnp.where(causal, logits + decay, -jnp.inf)
    p = jax.nn.softmax(logits, axis=-1)
    return jnp.einsum('bhqk,bkhd->bqhd', p.astype(v.dtype), v)


def stick_break_attend(q, k, v, q_pos, k_pos):
    z = jnp.einsum('bqhd,bkhd->bhqk', q.astype(jnp.float32), k.astype(jnp.float32)) * (HEAD_DIM ** -0.5)
    before = k_pos[None, :] < q_pos[:, None]
    log_keep = jnp.where(before, -jax.nn.softplus(z), 0.0)
    later = lax.cumsum(log_keep, axis=3, reverse=True) - log_keep
    w = jnp.where(before, jnp.exp(jax.nn.log_sigmoid(z) + later), 0.0)
    return jnp.einsum('bhqk,bkhd->bqhd', w.astype(v.dtype), v)


def merge_branches(o_a, o_b, gates, w_br_a, w_br_b, w_out):
    g_a, g_b, r_a, r_b = gates
    B, T = o_a.shape[:2]
    u_a = jnp.einsum('btw,wd->btd', o_a.reshape(B, T, WIDTH_FOX) * jax.nn.silu(g_a), w_br_a)
    u_b = jnp.einsum('btw,wd->btd', o_b.reshape(B, T, WIDTH_SB) * jax.nn.silu(g_b), w_br_b)
    m = jax.nn.sigmoid(r_a) * u_a + jax.nn.sigmoid(r_b) * u_b
    return jnp.einsum('btd,de->bte', m, w_out)


def prompt_layer(x, gain, w_in, b_f, w_br_a, w_br_b, w_out):
    B, T = x.shape[:2]
    h = rmsnorm(x, gain)
    (q_a, k_a, v_a, logf), (q_b, k_b, v_b), gates = project_inputs(h, w_in, b_f)
    sfx = suffix_exclusive(logf, axis=1)
    k_pos = jnp.arange(T)

    def block(i):
        start = i * BLOCK_Q
        sl = lambda t: lax.dynamic_slice_in_dim(t, start, BLOCK_Q, axis=1)
        q_pos = start + jnp.arange(BLOCK_Q)
        o_a = fox_attend(sl(q_a), k_a, v_a, sl(sfx), sfx, q_pos, k_pos)
        o_b = stick_break_attend(sl(q_b), k_b, v_b, q_pos, k_pos)
        return o_a, o_b

    o_a, o_b = lax.map(block, jnp.arange(T // BLOCK_Q))
    unblock = lambda o: jnp.moveaxis(o, 0, 1).reshape(B, T, o.shape[-2], o.shape[-1])
    y = merge_branches(unblock(o_a), unblock(o_b), gates, w_br_a, w_br_b, w_out)
    return x + y, (k_a, v_a, logf, k_b, v_b)


def sample_layer(x, l, cache_k_fox, cache_v_fox, cache_logf_fox, cache_k_sb, cache_v_sb, page_table,
                 gain, w_in, b_f, w_br_a, w_br_b, w_out):
    T = x.shape[1]
    past = page_table.shape[1] * PAGE_SIZE
    h = rmsnorm(x, gain)
    (q_a, k_a, v_a, logf), (q_b, k_b, v_b), gates = project_inputs(h, w_in, b_f)
    k_pos = jnp.arange(past + T)
    q_pos = past + jnp.arange(T)

    def one_seq(args):
        pages, qa, ka, va, lf, qb, kb, vb = args

        def gather(cache, new):
            rows = cache[l, pages]
            rows = rows.reshape((past,) + rows.shape[2:])
            return jnp.concatenate([rows.astype(new.dtype), new], axis=0)[None]

        lf_all = gather(cache_logf_fox, lf)
        sfx = suffix_exclusive(lf_all.astype(jnp.float32), axis=1)
        o_a = fox_attend(qa[None], gather(cache_k_fox, ka), gather(cache_v_fox, va),
                         sfx[:, past:], sfx, q_pos, k_pos)[0]
        o_b = stick_break_attend(qb[None], gather(cache_k_sb, kb), gather(cache_v_sb, vb), q_pos, k_pos)[0]
        return o_a, o_b

    o_a, o_b = lax.map(one_seq, (page_table, q_a, k_a, v_a, logf, q_b, k_b, v_b))
    y = merge_branches(o_a, o_b, gates, w_br_a, w_br_b, w_out)
    return x + y, (k_a, v_a, logf, k_b, v_b)


def setup_inputs(seed: int = 0) -> dict:
    key = jax.random.key(seed)
    ks = jax.random.split(key, 16)
    n_pages = PAST_LEN // PAGE_SIZE
    n_used = DEC_BATCH * n_pages
    n_phys = n_used + max(1, n_used // 4)
    perm = jax.random.permutation(ks[0], n_phys)
    page_table = perm[:n_used].reshape(DEC_BATCH, n_pages).astype(jnp.int32)
    f32 = jnp.float32
    kv_fox = (DEPTH, n_phys, PAGE_SIZE, N_HEADS_FOX, HEAD_DIM)
    kv_sb = (DEPTH, n_phys, PAGE_SIZE, N_HEADS_SB, HEAD_DIM)
    return {
        "x_prompt": jax.random.normal(ks[1], (BATCH, SEQ, D_MODEL), f32),
        "x_sample": jax.random.normal(ks[2], (DEC_BATCH, DEC_SEQ, D_MODEL), f32),
        "cache_k_fox": jax.random.normal(ks[3], kv_fox, f32),
        "cache_v_fox": jax.random.normal(ks[4], kv_fox, f32),
        "cache_logf_fox": jax.nn.log_sigmoid(FORGET_BIAS + jax.random.normal(ks[5], (DEPTH, n_phys, PAGE_SIZE, N_HEADS_FOX), f32)),
        "cache_k_sb": jax.random.normal(ks[6], kv_sb, f32),
        "cache_v_sb": jax.random.normal(ks[7], kv_sb, f32),
        "page_table": page_table,
        "norm_gain": 1.0 + 0.02 * jax.random.normal(ks[8], (DEPTH, D_MODEL), f32),
        "w_in": jax.random.normal(ks[9], (DEPTH, D_MODEL, PROJ_WIDTH), f32) * D_MODEL ** -0.5,
        "b_forget": FORGET_BIAS + 0.1 * jax.random.normal(ks[10], (DEPTH, N_HEADS_FOX), f32),
        "w_branch_fox": jax.random.normal(ks[11], (DEPTH, WIDTH_FOX, D_MODEL), f32) * WIDTH_FOX ** -0.5,
        "w_branch_sb": jax.random.normal(ks[12], (DEPTH, WIDTH_SB, D_MODEL), f32) * WIDTH_SB ** -0.5,
        "w_out": jax.random.normal(ks[13], (DEPTH, D_MODEL, D_MODEL), f32) * D_MODEL ** -0.5,
        "final_norm_gain": 1.0 + 0.02 * jax.random.normal(ks[14], (D_MODEL,), f32),
    }


def reference(x_prompt, x_sample, cache_k_fox, cache_v_fox, cache_logf_fox, cache_k_sb, cache_v_sb, page_table,
              norm_gain, w_in, b_forget, w_branch_fox, w_branch_sb, w_out, final_norm_gain):
    xp, xs = x_prompt, x_sample
    new_p = [[], [], [], [], []]
    new_s = [[], [], [], [], []]
    for l in range(DEPTH):
        xp, st_p = prompt_layer(xp, norm_gain[l], w_in[l], b_forget[l], w_branch_fox[l], w_branch_sb[l], w_out[l])
        xs, st_s = sample_layer(xs, l, cache_k_fox, cache_v_fox, cache_logf_fox, cache_k_sb, cache_v_sb, page_table,
                                norm_gain[l], w_in[l], b_forget[l], w_branch_fox[l], w_branch_sb[l], w_out[l])
        for i in range(5):
            new_p[i].append(st_p[i])
            new_s[i].append(st_s[i])
    y_prompt = rmsnorm(xp, final_norm_gain)
    y_sample = rmsnorm(xs, final_norm_gain)
    k_fox_p, v_fox_p, logf_p, k_sb_p, v_sb_p = [jnp.stack(t, axis=0) for t in new_p]
    k_fox_s, v_fox_s, logf_s, k_sb_s, v_sb_s = [jnp.stack(t, axis=0) for t in new_s]
    return (y_prompt, y_sample, k_fox_p, v_fox_p, logf_p, k_sb_p, v_sb_p, k_fox_s, v_fox_s, logf_s, k_sb_s, v_sb_s)
```

```python
import functools

import jax
import jax.numpy as jnp
from jax import lax
from jax.experimental import pallas as pl
from jax.experimental.pallas import tpu as pltpu

HEAD_DIM = 128
RMS_EPS = 1e-6
VMEM_LIMIT_BYTES = 56 * 1024 * 1024
F32 = jnp.float32
BF16 = jnp.bfloat16
NT_DIMS = (((1,), (1,)), ((), ()))


def _cparams(sem):
    return pltpu.CompilerParams(dimension_semantics=sem, vmem_limit_bytes=VMEM_LIMIT_BYTES)


def _log_sigmoid(u):
    return jnp.minimum(u, 0.0) - jnp.log1p(jnp.exp(-jnp.abs(u)))


def _strict_lower_ones(n, dtype):
    row = lax.broadcasted_iota(jnp.int32, (n, n), 0)
    col = lax.broadcasted_iota(jnp.int32, (n, n), 1)
    return jnp.where(row > col, 1.0, 0.0).astype(dtype)


def _dot_split(x, m01, terms):
    acc = None
    rem = x
    for i in range(terms):
        piece = rem.astype(BF16)
        part = jnp.dot(piece, m01, preferred_element_type=F32)
        acc = part if acc is None else acc + part
        if i + 1 < terms:
            rem = rem - piece.astype(F32)
    return acc


def _norm_kernel(x_ref, gain_ref, wft_ref, bf_ref, h_ref, lft_ref):
    x = x_ref[...]
    ms = jnp.mean(x * x, axis=-1, keepdims=True)
    hb = (x * lax.rsqrt(ms + RMS_EPS) * gain_ref[...]).astype(BF16)
    h_ref[...] = hb
    f = lax.dot_general(wft_ref[...], hb, NT_DIMS, preferred_element_type=F32)
    lft_ref[...] = _log_sigmoid(f + bf_ref[...])


def _norm(x2d, gain, wft, bf, tm):
    m, d = x2d.shape
    nh = wft.shape[0]
    return pl.pallas_call(
        _norm_kernel,
        grid=(m // tm,),
        in_specs=[
            pl.BlockSpec((tm, d), lambda i: (i, 0)),
            pl.BlockSpec((1, d), lambda i: (0, 0)),
            pl.BlockSpec((nh, d), lambda i: (0, 0)),
            pl.BlockSpec((nh, 1), lambda i: (0, 0)),
        ],
        out_specs=[
            pl.BlockSpec((tm, d), lambda i: (i, 0)),
            pl.BlockSpec((nh, tm), lambda i: (0, i)),
        ],
        out_shape=[
            jax.ShapeDtypeStruct((m, d), BF16),
            jax.ShapeDtypeStruct((nh, m), F32),
        ],
        compiler_params=_cparams(("parallel",)),
        name="norm",
    )(x2d, gain, wft, bf)


def _mm_kernel(h_ref, w_ref, o_ref):
    o_ref[...] = jnp.dot(h_ref[...], w_ref[...], preferred_element_type=F32)


def _matmul(h, w, col_block0, n, tm, tn, name):
    m, k = h.shape
    return pl.pallas_call(
        _mm_kernel,
        grid=(m // tm, n // tn),
        in_specs=[
            pl.BlockSpec((tm, k), lambda i, j: (i, 0)),
            pl.BlockSpec((k, tn), lambda i, j: (0, col_block0 + j)),
        ],
        out_specs=pl.BlockSpec((tm, tn), lambda i, j: (i, j)),
        out_shape=jax.ShapeDtypeStruct((m, n), F32),
        compiler_params=_cparams(("parallel", "parallel")),
        name=name,
    )(h, w)


def _sfx_kernel(lf_ref, o_ref, *, blk):
    t = lf_ref.shape[1]
    m01 = _strict_lower_ones(blk, BF16)
    carry = jnp.zeros((lf_ref.shape[0], 1), F32)
    for c in reversed(range(t // blk)):
        x = lf_ref[:, c * blk:(c + 1) * blk]
        o_ref[:, c * blk:(c + 1) * blk] = _dot_split(x, m01, 3) + carry
        carry = carry + jnp.sum(x, axis=-1, keepdims=True)


def _suffix_sums(lft, seq):
    nh, m = lft.shape
    return pl.pallas_call(
        functools.partial(_sfx_kernel, blk=256),
        grid=(m // seq,),
        in_specs=[pl.BlockSpec((nh, seq), lambda b: (0, b))],
        out_specs=pl.BlockSpec((nh, seq), lambda b: (0, b)),
        out_shape=jax.ShapeDtypeStruct((nh, m), F32),
        compiler_params=_cparams(("parallel",)),
        name="sfx",
    )(lft)


def _fox_kernel(q_ref, k_ref, v_ref, sfx_ref, o_ref, *, tq, scale):
    qi = pl.program_id(2)
    q = (q_ref[...] * scale).astype(BF16)

    def step(kb, carry, diagonal):
        m_run, l_run, acc = carry
        off = pl.multiple_of(kb * tq, tq)
        k = k_ref[pl.ds(off, tq), :].astype(BF16)
        v = v_ref[pl.ds(off, tq), :].astype(BF16)
        s = lax.dot_general(q, k, NT_DIMS, preferred_element_type=F32)
        s = s + sfx_ref[:, pl.ds(off, tq)]
        if diagonal:
            row = lax.broadcasted_iota(jnp.int32, (tq, tq), 0)
            col = lax.broadcasted_iota(jnp.int32, (tq, tq), 1)
            s = jnp.where(col <= row, s, -jnp.inf)
        m_new = jnp.maximum(m_run, jnp.max(s, axis=-1, keepdims=True))
        p = jnp.exp(s - m_new)
        alpha = jnp.exp(m_run - m_new)
        l_new = alpha * l_run + jnp.sum(p, axis=-1, keepdims=True)
        acc = alpha * acc + jnp.dot(p.astype(BF16), v, preferred_element_type=F32)
        return m_new, l_new, acc

    init = (jnp.full((tq, 1), -jnp.inf, F32), jnp.zeros((tq, 1), F32), jnp.zeros((tq, HEAD_DIM), F32))
    carry = lax.fori_loop(0, qi, lambda kb, c: step(kb, c, False), init)
    _, l_fin, acc = step(qi, carry, True)
    o_ref[...] = acc / l_fin


def _fox_prompt(q, k, v, sfx3, batch, seq, tq):
    m, width = q.shape
    nh = width // HEAD_DIM
    nq = seq // tq
    return pl.pallas_call(
        functools.partial(_fox_kernel, tq=tq, scale=HEAD_DIM ** -0.5),
        grid=(batch, nh, nq),
        in_specs=[
            pl.BlockSpec((tq, HEAD_DIM), lambda b, h, i: (b * nq + i, h)),
            pl.BlockSpec((seq, HEAD_DIM), lambda b, h, i: (b, h)),
            pl.BlockSpec((seq, HEAD_DIM), lambda b, h, i: (b, h)),
            pl.BlockSpec((None, 1, seq), lambda b, h, i: (h, 0, b)),
        ],
        out_specs=pl.BlockSpec((tq, HEAD_DIM), lambda b, h, i: (b * nq + i, h)),
        out_shape=jax.ShapeDtypeStruct((m, width), F32),
        compiler_params=_cparams(("parallel", "parallel", "arbitrary")),
        name="fox_p",
    )(q, k, v, sfx3)


def _sb_scores(z):
    l1 = jnp.log1p(jnp.exp(-jnp.abs(z)))
    return -(jnp.maximum(z, 0.0) + l1), jnp.minimum(z, 0.0) - l1


def _sb_kernel(q_ref, k_ref, v_ref, o_ref, *, tq, scale):
    qi = pl.program_id(2)
    q = (q_ref[...] * scale).astype(BF16)
    m01 = _strict_lower_ones(tq, BF16)

    def step(kb, carry, diagonal):
        run, acc = carry
        off = pl.multiple_of(kb * tq, tq)
        k = k_ref[pl.ds(off, tq), :].astype(BF16)
        v = v_ref[pl.ds(off, tq), :].astype(BF16)
        z = lax.dot_general(q, k, NT_DIMS, preferred_element_type=F32)
        log_keep, log_beta = _sb_scores(z)
        if diagonal:
            row = lax.broadcasted_iota(jnp.int32, (tq, tq), 0)
            col = lax.broadcasted_iota(jnp.int32, (tq, tq), 1)
            before = col < row
            log_keep = jnp.where(before, log_keep, 0.0)
        later = _dot_split(log_keep, m01, 2) + run
        w = jnp.exp(log_beta + later)
        if diagonal:
            w = jnp.where(before, w, 0.0)
        acc = acc + jnp.dot(w.astype(BF16), v, preferred_element_type=F32)
        run = run + jnp.sum(log_keep, axis=-1, keepdims=True)
        return run, acc

    carry = step(qi, (jnp.zeros((tq, 1), F32), jnp.zeros((tq, HEAD_DIM), F32)), True)
    carry = lax.fori_loop(0, qi, lambda i, c: step(qi - 1 - i, c, False), carry)
    o_ref[...] = carry[1]


def _sb_prompt(q, k, v, batch, seq, tq):
    m, width = q.shape
    nh = width // HEAD_DIM
    nq = seq // tq
    return pl.pallas_call(
        functools.partial(_sb_kernel, tq=tq, scale=HEAD_DIM ** -0.5),
        grid=(batch, nh, nq),
        in_specs=[
            pl.BlockSpec((tq, HEAD_DIM), lambda b, h, i: (b * nq + i, h)),
            pl.BlockSpec((seq, HEAD_DIM), lambda b, h, i: (b, h)),
            pl.BlockSpec((seq, HEAD_DIM), lambda b, h, i: (b, h)),
        ],
        out_specs=pl.BlockSpec((tq, HEAD_DIM), lambda b, h, i: (b * nq + i, h)),
        out_shape=jax.ShapeDtypeStruct((m, width), F32),
        compiler_params=_cparams(("parallel", "parallel", "arbitrary")),
        name="sb_p",
    )(q, k, v)


def _merge_kernel(oa_ref, ga_ref, ob_ref, gb_ref, ra_ref, rb_ref, x_ref, wa_ref, wb_ref, wo_ref, gain_ref, y_ref):
    ga = ga_ref[...]
    gb = gb_ref[...]
    a = (oa_ref[...] * (ga * jax.nn.sigmoid(ga))).astype(BF16)
    b = (ob_ref[...] * (gb * jax.nn.sigmoid(gb))).astype(BF16)
    ua = jnp.dot(a, wa_ref[...], preferred_element_type=F32)
    ub = jnp.dot(b, wb_ref[...], preferred_element_type=F32)
    mix = (jax.nn.sigmoid(ra_ref[...]) * ua + jax.nn.sigmoid(rb_ref[...]) * ub).astype(BF16)
    res = x_ref[...] + jnp.dot(mix, wo_ref[...], preferred_element_type=F32)
    ms = jnp.mean(res * res, axis=-1, keepdims=True)
    y_ref[...] = res * lax.rsqrt(ms + RMS_EPS) * gain_ref[...]


def _merge(oa, ga, ob, gb, ra, rb, x2d, wa, wb, wo, gain, tm):
    m, d = x2d.shape
    width = oa.shape[1]
    row = lambda w: pl.BlockSpec((tm, w), lambda i: (i, 0))
    const = lambda s: pl.BlockSpec(s, lambda i: (0, 0), pipeline_mode=pl.Buffered(1))
    return pl.pallas_call(
        _merge_kernel,
        grid=(m // tm,),
        in_specs=[row(width), row(width), row(width), row(width), row(d), row(d), row(d),
                  const((width, d)), const((width, d)), const((d, d)), const((1, d))],
        out_specs=row(d),
        out_shape=jax.ShapeDtypeStruct((m, d), F32),
        compiler_params=_cparams(("parallel",)),
        name="merge",
    )(oa, ga, ob, gb, ra, rb, x2d, wa, wb, wo, gain)


def _lfaux_kernel(x_ref, o_ref):
    pb = o_ref.shape[0]
    nh = x_ref.shape[0] // pb
    n = x_ref.shape[1]
    x = x_ref[...]
    sfx = _dot_split(x, _strict_lower_ones(n, BF16), 3)
    tot = _dot_split(x, jnp.ones((n, n), BF16), 3)
    o_ref[:, 0:nh, :] = sfx.reshape(pb, nh, n)
    o_ref[:, nh:2 * nh, :] = tot.reshape(pb, nh, n)


def _logf_page_aux(lf_t2d, nh, pb):
    rows, page = lf_t2d.shape
    n_phys = rows // nh
    return pl.pallas_call(
        _lfaux_kernel,
        grid=(n_phys // pb,),
        in_specs=[pl.BlockSpec((pb * nh, page), lambda i: (i, 0))],
        out_specs=pl.BlockSpec((pb, 2 * nh, page), lambda i: (i, 0, 0)),
        out_shape=jax.ShapeDtypeStruct((n_phys, 2 * nh, page), F32),
        compiler_params=_cparams(("parallel",)),
        name="lfaux",
    )(lf_t2d)


def _dec_kernel(pt_ref, qa_ref, kan_ref, van_ref, qb_ref, kbn_ref, vbn_ref, lfn_ref,
                kf_hbm, vf_hbm, ks_hbm, vs_hbm, aux_hbm,
                oa_ref, ob_ref,
                kf_buf, vf_buf, ks_buf, vs_buf, aux_buf, sem,
                qa_bd, qb_bd, new_pad, m_a, l_a, acc_a, run_f, acc_b, run_s,
                *, n_pages, chunk, scale):
    b = pl.program_id(0)
    c = pl.program_id(1)
    nb = pl.num_programs(0)
    nc = pl.num_programs(1)
    step = b * nc + c
    slot = step % 2
    nt, width = qa_ref.shape
    nh = width // HEAD_DIM
    rows = nt * nh
    page = kf_hbm.shape[1]
    ck = chunk * page
    caches = ((kf_hbm, kf_buf), (vf_hbm, vf_buf), (ks_hbm, ks_buf), (vs_hbm, vs_buf))

    def page_copies(step_idx, slot_idx):
        sb = step_idx // nc
        sc = step_idx % nc
        base = sb * n_pages + (n_pages - (sc + 1) * chunk)
        copies = []
        for i in range(chunk):
            pid = pt_ref[base + i]
            for ci, (src, dst) in enumerate(caches):
                copies.append(pltpu.make_async_copy(
                    src.at[pid], dst.at[slot_idx, pl.ds(i * page, page), :], sem.at[slot_idx, ci]))
            copies.append(pltpu.make_async_copy(aux_hbm.at[pid], aux_buf.at[slot_idx, i], sem.at[slot_idx, 4]))
        return copies

    @pl.when(step == 0)
    def _():
        for cp in page_copies(step, slot):
            cp.start()
        new_pad[...] = jnp.zeros(new_pad.shape, F32)

    @pl.when(step + 1 < nb * nc)
    def _():
        for cp in page_copies(step + 1, 1 - slot):
            cp.start()

    head_of_col = lax.broadcasted_iota(jnp.int32, (nh, width), 1) // HEAD_DIM
    head_mask = jnp.where(head_of_col == lax.broadcasted_iota(jnp.int32, (nh, width), 0), 1.0, 0.0)

    def tile_rows(x8):
        return jnp.concatenate([x8] * nt, axis=0)

    @pl.when(c == 0)
    def _():
        def block_diag(q_ref):
            q = q_ref[...] * scale
            return jnp.concatenate([q[t:t + 1, :] * head_mask for t in range(nt)], axis=0).astype(BF16)

        qa_bd[...] = block_diag(qa_ref)
        qb_bd[...] = block_diag(qb_ref)
        tok = lax.broadcasted_iota(jnp.int32, (rows, page), 0) // nh
        key = lax.broadcasted_iota(jnp.int32, (rows, page), 1)
        m01 = _strict_lower_ones(page, BF16)

        def padded(ref, idx):
            new_pad[idx, 0:nt, :] = ref[...]
            return new_pad[idx].astype(BF16)

        lfn = lfn_ref[...]
        s = lax.dot_general(qa_bd[...], padded(kan_ref, 0), NT_DIMS, preferred_element_type=F32)
        s = jnp.where(key <= tok, s, -jnp.inf) + tile_rows(_dot_split(lfn, m01, 3))
        m_new = jnp.max(s, axis=-1, keepdims=True)
        p = jnp.exp(s - m_new)
        m_a[...] = m_new
        l_a[...] = jnp.sum(p, axis=-1, keepdims=True)
        acc_a[...] = jnp.dot(p.astype(BF16), padded(van_ref, 1), preferred_element_type=F32)
        run_f[...] = jnp.broadcast_to(jnp.sum(lfn, axis=-1, keepdims=True), run_f.shape)

        z = lax.dot_general(qb_bd[...], padded(kbn_ref, 2), NT_DIMS, preferred_element_type=F32)
        log_keep, log_beta = _sb_scores(z)
        before = key < tok
        log_keep = jnp.where(before, log_keep, 0.0)
        w = jnp.where(before, jnp.exp(log_beta + _dot_split(log_keep, m01, 2)), 0.0)
        acc_b[...] = jnp.dot(w.astype(BF16), padded(vbn_ref, 3), preferred_element_type=F32)
        run_s[...] = jnp.sum(log_keep, axis=-1, keepdims=True)

    for cp in page_copies(step, slot):
        cp.wait()

    kc = kf_buf[slot].astype(BF16)
    s = lax.dot_general(qa_bd[...], kc, NT_DIMS, preferred_element_type=F32)
    run = run_f[...]
    bias = [None] * chunk
    for i in reversed(range(chunk)):
        bias[i] = aux_buf[slot, i, 0:nh, :] + run
        run = run + aux_buf[slot, i, nh:2 * nh, :]
    run_f[...] = run
    s = s + tile_rows(jnp.concatenate(bias, axis=1))
    m_old = m_a[...]
    m_new = jnp.maximum(m_old, jnp.max(s, axis=-1, keepdims=True))
    p = jnp.exp(s - m_new)
    alpha = jnp.exp(m_old - m_new)
    m_a[...] = m_new
    l_a[...] = alpha * l_a[...] + jnp.sum(p, axis=-1, keepdims=True)
    acc_a[...] = alpha * acc_a[...] + jnp.dot(p.astype(BF16), vf_buf[slot].astype(BF16), preferred_element_type=F32)

    z = lax.dot_general(qb_bd[...], ks_buf[slot].astype(BF16), NT_DIMS, preferred_element_type=F32)
    log_keep, log_beta = _sb_scores(z)
    later = _dot_split(log_keep, _strict_lower_ones(ck, BF16), 2) + run_s[...]
    w = jnp.exp(log_beta + later)
    acc_b[...] = acc_b[...] + jnp.dot(w.astype(BF16), vs_buf[slot].astype(BF16), preferred_element_type=F32)
    run_s[...] = run_s[...] + jnp.sum(log_keep, axis=-1, keepdims=True)

    @pl.when(c == nc - 1)
    def _():
        fa = acc_a[...] / l_a[...]
        fb = acc_b[...]
        for t in range(nt):
            oa_ref[t:t + 1, :] = jnp.sum(fa[t * nh:(t + 1) * nh, :] * head_mask, axis=0, keepdims=True)
            ob_ref[t:t + 1, :] = jnp.sum(fb[t * nh:(t + 1) * nh, :] * head_mask, axis=0, keepdims=True)


def _decode_attention(page_table, qa, kan, van, qb, kbn, vbn, lfn, kf, vf, ks, vs, aux, chunk):
    nb, nt, width = qa.shape
    nh = width // HEAD_DIM
    n_pages = page_table.shape[1]
    page = kf.shape[1]
    rows = nt * nh
    nc = n_pages // chunk
    tok = pl.BlockSpec((None, nt, width), lambda b, c, pt: (b, 0, 0))
    hbm = pl.BlockSpec(memory_space=pl.ANY)
    grid_spec = pltpu.PrefetchScalarGridSpec(
        num_scalar_prefetch=1,
        grid=(nb, nc),
        in_specs=[tok, tok, tok, tok, tok, tok,
                  pl.BlockSpec((None, nh, page), lambda b, c, pt: (b, 0, 0)),
                  hbm, hbm, hbm, hbm, hbm],
        out_specs=[tok, tok],
        scratch_shapes=[
            pltpu.VMEM((2, chunk * page, width), F32),
            pltpu.VMEM((2, chunk * page, width), F32),
            pltpu.VMEM((2, chunk * page, width), F32),
            pltpu.VMEM((2, chunk * page, width), F32),
            pltpu.VMEM((2, chunk, 2 * nh, page), F32),
            pltpu.SemaphoreType.DMA((2, 5)),
            pltpu.VMEM((rows, width), BF16),
            pltpu.VMEM((rows, width), BF16),
            pltpu.VMEM((4, page, width), F32),
            pltpu.VMEM((rows, 1), F32),
            pltpu.VMEM((rows, 1), F32),
            pltpu.VMEM((rows, width), F32),
            pltpu.VMEM((nh, page), F32),
            pltpu.VMEM((rows, width), F32),
            pltpu.VMEM((rows, 1), F32),
        ],
    )
    return pl.pallas_call(
        functools.partial(_dec_kernel, n_pages=n_pages, chunk=chunk, scale=HEAD_DIM ** -0.5),
        grid_spec=grid_spec,
        out_shape=[jax.ShapeDtypeStruct((nb, nt, width), F32)] * 2,
        compiler_params=_cparams(("arbitrary", "arbitrary")),
        name="dec",
    )(page_table.reshape(-1), qa, kan, van, qb, kbn, vbn, lfn, kf, vf, ks, vs, aux)


def _project(x2d, gain, w_main, w_r, wft, bf, width, tm_norm, tm):
    d = x2d.shape[1]
    h, lft = _norm(x2d, gain, wft, bf, tm_norm)
    groups = [_matmul(h, w_main, g, width, tm, width, "proj_main") for g in range(8)]
    r_a = _matmul(h, w_r, 0, d, tm, width, "proj_gate")
    r_b = _matmul(h, w_r, d // width, d, tm, width, "proj_gate")
    return groups, r_a, r_b, lft


def kernel(x_prompt, x_sample, cache_k_fox, cache_v_fox, cache_logf_fox, cache_k_sb, cache_v_sb, page_table,
           norm_gain, w_in, b_forget, w_branch_fox, w_branch_sb, w_out, final_norm_gain):
    depth = norm_gain.shape[0]
    assert depth == 1, "single-layer step"
    batch, seq, d = x_prompt.shape
    nb, nt, _ = x_sample.shape
    nh = b_forget.shape[1]
    width = nh * HEAD_DIM
    n_phys, page = cache_k_fox.shape[1], cache_k_fox.shape[2]

    w = w_in[0]
    w_main = w[:, :8 * width].astype(BF16)
    wft = w[:, 8 * width:8 * width + nh].T.astype(BF16)
    w_r = w[:, 8 * width + nh:].astype(BF16)
    bf = b_forget[0].reshape(nh, 1)
    gain = norm_gain[0].reshape(1, d)
    fgain = final_norm_gain.reshape(1, d)
    wa = w_branch_fox[0].astype(BF16)
    wb = w_branch_sb[0].astype(BF16)
    wo = w_out[0].astype(BF16)

    xp = x_prompt.reshape(batch * seq, d)
    (qa, ka, va, ga, qb, kb, vb, gb), ra, rb, lft_p = _project(xp, gain, w_main, w_r, wft, bf, width, 256, 512)
    sfx = _suffix_sums(lft_p, seq).reshape(nh, 1, batch * seq)
    oa = _fox_prompt(qa, ka, va, sfx, batch, seq, 256)
    ob = _sb_prompt(qb, kb, vb, batch, seq, 256)
    y_p = _merge(oa, ga, ob, gb, ra, rb, xp, wa, wb, wo, fgain, 256)

    xs = x_sample.reshape(nb * nt, d)
    (qa_s, ka_s, va_s, ga_s, qb_s, kb_s, vb_s, gb_s), ra_s, rb_s, lft_s = _project(
        xs, gain, w_main, w_r, wft, bf, width, nb * nt, nb * nt)
    lf_t = jnp.swapaxes(cache_logf_fox[0], 1, 2).reshape(n_phys * nh, page)
    aux = _logf_page_aux(lf_t, nh, 64)
    lfn = jnp.pad(jnp.transpose(lft_s.reshape(nh, nb, nt), (1, 0, 2)), ((0, 0), (0, 0), (0, page - nt)))
    tok3 = lambda t: t.reshape(nb, nt, width)
    cache2 = lambda t: t.reshape(n_phys, page, width)
    oa_s, ob_s = _decode_attention(
        page_table, tok3(qa_s), tok3(ka_s), tok3(va_s), tok3(qb_s), tok3(kb_s), tok3(vb_s), lfn,
        cache2(cache_k_fox), cache2(cache_v_fox), cache2(cache_k_sb), cache2(cache_v_sb), aux, 4)
    y_s = _merge(oa_s.reshape(nb * nt, width), ga_s, ob_s.reshape(nb * nt, width), gb_s, ra_s, rb_s, xs,
                 wa, wb, wo, fgain, nb * nt)

    heads_p = lambda t: t.reshape(1, batch, seq, nh, HEAD_DIM)
    heads_s = lambda t: t.reshape(1, nb, nt, nh, HEAD_DIM)
    return (
        y_p.reshape(batch, seq, d), y_s.reshape(nb, nt, d),
        heads_p(ka), heads_p(va), lft_p.T.reshape(1, batch, seq, nh), heads_p(kb), heads_p(vb),
        heads_s(ka_s), heads_s(va_s), lft_s.T.reshape(1, nb, nt, nh), heads_s(kb_s), heads_s(vb_s),
    )
```

```python
import functools

import jax
import jax.numpy as jnp
from jax import lax
from jax.experimental import pallas as pl
from jax.experimental.pallas import tpu as pltpu

HEAD_DIM = 128
RMS_EPS = 1e-6
VMEM_LIMIT_BYTES = 56 * 1024 * 1024
SUBLANES = 8
F32 = jnp.float32
BF16 = jnp.bfloat16
NT_DIMS = (((1,), (1,)), ((), ()))


def _cparams(sem):
    return pltpu.CompilerParams(dimension_semantics=sem, vmem_limit_bytes=VMEM_LIMIT_BYTES)


def _log_sigmoid(u):
    return jnp.minimum(u, 0.0) - jnp.log1p(jnp.exp(-jnp.abs(u)))


def _strict_lower_ones(n, dtype):
    row = lax.broadcasted_iota(jnp.int32, (n, n), 0)
    col = lax.broadcasted_iota(jnp.int32, (n, n), 1)
    return jnp.where(row > col, 1.0, 0.0).astype(dtype)


def _dot_split(x, m01, terms):
    acc = None
    rem = x
    for i in range(terms):
        piece = rem.astype(BF16)
        part = jnp.dot(piece, m01, preferred_element_type=F32)
        acc = part if acc is None else acc + part
        if i + 1 < terms:
            rem = rem - piece.astype(F32)
    return acc


def _norm_kernel(x_ref, gain_ref, wft_ref, bf_ref, h_ref, lft_ref):
    x = x_ref[...]
    ms = jnp.mean(x * x, axis=-1, keepdims=True)
    hb = (x * lax.rsqrt(ms + RMS_EPS) * gain_ref[...]).astype(BF16)
    h_ref[...] = hb
    f = lax.dot_general(wft_ref[...], hb, NT_DIMS, preferred_element_type=F32)
    lft_ref[...] = _log_sigmoid(f + bf_ref[...])


def _norm(x2d, gain, wft, bf, tm):
    m, d = x2d.shape
    nh = wft.shape[0]
    return pl.pallas_call(
        _norm_kernel,
        grid=(m // tm,),
        in_specs=[
            pl.BlockSpec((tm, d), lambda i: (i, 0)),
            pl.BlockSpec((1, d), lambda i: (0, 0)),
            pl.BlockSpec((nh, d), lambda i: (0, 0)),
            pl.BlockSpec((nh, 1), lambda i: (0, 0)),
        ],
        out_specs=[
            pl.BlockSpec((tm, d), lambda i: (i, 0)),
            pl.BlockSpec((nh, tm), lambda i: (0, i)),
        ],
        out_shape=[
            jax.ShapeDtypeStruct((m, d), BF16),
            jax.ShapeDtypeStruct((nh, m), F32),
        ],
        compiler_params=_cparams(("parallel",)),
        name="norm",
    )(x2d, gain, wft, bf)


def _mm_kernel(h_ref, w_ref, o_ref):
    o_ref[...] = jnp.dot(h_ref[...], w_ref[...], preferred_element_type=F32)


def _matmul(h, w, col_block0, n, tm, tn, name):
    m, k = h.shape
    return pl.pallas_call(
        _mm_kernel,
        grid=(m // tm, n // tn),
        in_specs=[
            pl.BlockSpec((tm, k), lambda i, j: (i, 0)),
            pl.BlockSpec((k, tn), lambda i, j: (0, col_block0 + j)),
        ],
        out_specs=pl.BlockSpec((tm, tn), lambda i, j: (i, j)),
        out_shape=jax.ShapeDtypeStruct((m, n), F32),
        compiler_params=_cparams(("parallel", "parallel")),
        name=name,
    )(h, w)


def _sfx_kernel(lf_ref, o_ref, *, blk):
    t = lf_ref.shape[1]
    m01 = _strict_lower_ones(blk, BF16)
    carry = jnp.zeros((lf_ref.shape[0], 1), F32)
    for c in reversed(range(t // blk)):
        x = lf_ref[:, c * blk:(c + 1) * blk]
        o_ref[:, c * blk:(c + 1) * blk] = _dot_split(x, m01, 3) + carry
        carry = carry + jnp.sum(x, axis=-1, keepdims=True)


def _suffix_sums(lft, seq):
    nh, m = lft.shape
    return pl.pallas_call(
        functools.partial(_sfx_kernel, blk=256),
        grid=(m // seq,),
        in_specs=[pl.BlockSpec((nh, seq), lambda b: (0, b))],
        out_specs=pl.BlockSpec((nh, seq), lambda b: (0, b)),
        out_shape=jax.ShapeDtypeStruct((nh, m), F32),
        compiler_params=_cparams(("parallel",)),
        name="sfx",
    )(lft)


def _fox_kernel(q_ref, k_ref, v_ref, sfx_ref, o_ref, *, tq, scale):
    qi = pl.program_id(2)
    q = (q_ref[...] * scale).astype(BF16)

    def step(kb, carry, diagonal):
        m_run, l_run, acc = carry
        off = pl.multiple_of(kb * tq, tq)
        k = k_ref[pl.ds(off, tq), :].astype(BF16)
        v = v_ref[pl.ds(off, tq), :].astype(BF16)
        s = lax.dot_general(q, k, NT_DIMS, preferred_element_type=F32)
        s = s + sfx_ref[:, pl.ds(off, tq)]
        if diagonal:
            row = lax.broadcasted_iota(jnp.int32, (tq, tq), 0)
            col = lax.broadcasted_iota(jnp.int32, (tq, tq), 1)
            s = jnp.where(col <= row, s, -jnp.inf)
        m_new = jnp.maximum(m_run, jnp.max(s, axis=-1, keepdims=True))
        p = jnp.exp(s - m_new)
        alpha = jnp.exp(m_run - m_new)
        l_new = alpha * l_run + jnp.sum(p, axis=-1, keepdims=True)
        acc = alpha * acc + jnp.dot(p.astype(BF16), v, preferred_element_type=F32)
        return m_new, l_new, acc

    init = (jnp.full((tq, 1), -jnp.inf, F32), jnp.zeros((tq, 1), F32), jnp.zeros((tq, HEAD_DIM), F32))
    carry = lax.fori_loop(0, qi, lambda kb, c: step(kb, c, False), init)
    _, l_fin, acc = step(qi, carry, True)
    o_ref[...] = acc / l_fin


def _fox_prompt(q, k, v, sfx3, batch, seq, tq):
    m, width = q.shape
    nh = width // HEAD_DIM
    nq = seq // tq
    return pl.pallas_call(
        functools.partial(_fox_kernel, tq=tq, scale=HEAD_DIM ** -0.5),
        grid=(batch, nh, nq),
        in_specs=[
            pl.BlockSpec((tq, HEAD_DIM), lambda b, h, i: (b * nq + i, h)),
            pl.BlockSpec((seq, HEAD_DIM), lambda b, h, i: (b, h)),
            pl.BlockSpec((seq, HEAD_DIM), lambda b, h, i: (b, h)),
            pl.BlockSpec((None, 1, seq), lambda b, h, i: (h, 0, b)),
        ],
        out_specs=pl.BlockSpec((tq, HEAD_DIM), lambda b, h, i: (b * nq + i, h)),
        out_shape=jax.ShapeDtypeStruct((m, width), F32),
        compiler_params=_cparams(("parallel", "parallel", "arbitrary")),
        name="fox_p",
    )(q, k, v, sfx3)


def _sb_scores(z):
    l1 = jnp.log1p(jnp.exp(-jnp.abs(z)))
    return -(jnp.maximum(z, 0.0) + l1), jnp.minimum(z, 0.0) - l1


def _sb_kernel(q_ref, k_ref, v_ref, o_ref, *, tq, scale):
    qi = pl.program_id(2)
    q = (q_ref[...] * scale).astype(BF16)
    m01 = _strict_lower_ones(tq, BF16)

    def step(kb, carry, diagonal):
        run, acc = carry
        off = pl.multiple_of(kb * tq, tq)
        k = k_ref[pl.ds(off, tq), :].astype(BF16)
        v = v_ref[pl.ds(off, tq), :].astype(BF16)
        z = lax.dot_general(q, k, NT_DIMS, preferred_element_type=F32)
        log_keep, log_beta = _sb_scores(z)
        if diagonal:
            row = lax.broadcasted_iota(jnp.int32, (tq, tq), 0)
            col = lax.broadcasted_iota(jnp.int32, (tq, tq), 1)
            before = col < row
            log_keep = jnp.where(before, log_keep, 0.0)
        later = _dot_split(log_keep, m01, 2) + run
        w = jnp.exp(log_beta + later)
        if diagonal:
            w = jnp.where(before, w, 0.0)
        acc = acc + jnp.dot(w.astype(BF16), v, preferred_element_type=F32)
        run = run + jnp.sum(log_keep, axis=-1, keepdims=True)
        return run, acc

    carry = step(qi, (jnp.zeros((tq, 1), F32), jnp.zeros((tq, HEAD_DIM), F32)), True)
    carry = lax.fori_loop(0, qi, lambda i, c: step(qi - 1 - i, c, False), carry)
    o_ref[...] = carry[1]


def _sb_prompt(q, k, v, batch, seq, tq):
    m, width = q.shape
    nh = width // HEAD_DIM
    nq = seq // tq
    return pl.pallas_call(
        functools.partial(_sb_kernel, tq=tq, scale=HEAD_DIM ** -0.5),
        grid=(batch, nh, nq),
        in_specs=[
            pl.BlockSpec((tq, HEAD_DIM), lambda b, h, i: (b * nq + i, h)),
            pl.BlockSpec((seq, HEAD_DIM), lambda b, h, i: (b, h)),
            pl.BlockSpec((seq, HEAD_DIM), lambda b, h, i: (b, h)),
        ],
        out_specs=pl.BlockSpec((tq, HEAD_DIM), lambda b, h, i: (b * nq + i, h)),
        out_shape=jax.ShapeDtypeStruct((m, width), F32),
        compiler_params=_cparams(("parallel", "parallel", "arbitrary")),
        name="sb_p",
    )(q, k, v)


def _merge_kernel(oa_ref, ga_ref, ob_ref, gb_ref, ra_ref, rb_ref, x_ref, wa_ref, wb_ref, wo_ref, gain_ref, y_ref):
    ga = ga_ref[...]
    gb = gb_ref[...]
    a = (oa_ref[...] * (ga * jax.nn.sigmoid(ga))).astype(BF16)
    b = (ob_ref[...] * (gb * jax.nn.sigmoid(gb))).astype(BF16)
    ua = jnp.dot(a, wa_ref[...], preferred_element_type=F32)
    ub = jnp.dot(b, wb_ref[...], preferred_element_type=F32)
    mix = (jax.nn.sigmoid(ra_ref[...]) * ua + jax.nn.sigmoid(rb_ref[...]) * ub).astype(BF16)
    res = x_ref[...] + jnp.dot(mix, wo_ref[...], preferred_element_type=F32)
    ms = jnp.mean(res * res, axis=-1, keepdims=True)
    y_ref[...] = res * lax.rsqrt(ms + RMS_EPS) * gain_ref[...]


def _merge(oa, ga, ob, gb, ra, rb, x2d, wa, wb, wo, gain, tm):
    m, d = x2d.shape
    width = oa.shape[1]
    row = lambda w: pl.BlockSpec((tm, w), lambda i: (i, 0))
    const = lambda s: pl.BlockSpec(s, lambda i: (0, 0), pipeline_mode=pl.Buffered(1))
    return pl.pallas_call(
        _merge_kernel,
        grid=(m // tm,),
        in_specs=[row(width), row(width), row(width), row(width), row(d), row(d), row(d),
                  const((width, d)), const((width, d)), const((d, d)), const((1, d))],
        out_specs=row(d),
        out_shape=jax.ShapeDtypeStruct((m, d), F32),
        compiler_params=_cparams(("parallel",)),
        name="merge",
    )(oa, ga, ob, gb, ra, rb, x2d, wa, wb, wo, gain)


def _lfaux_kernel(x_ref, o_ref):
    pb = o_ref.shape[0]
    nh = x_ref.shape[0] // pb
    n = x_ref.shape[1]
    x = x_ref[...]
    sfx = _dot_split(x, _strict_lower_ones(n, BF16), 3)
    tot = _dot_split(x, jnp.ones((n, n), BF16), 3)
    o_ref[:, 0:nh, :] = sfx.reshape(pb, nh, n)
    o_ref[:, nh:2 * nh, :] = tot.reshape(pb, nh, n)


def _logf_page_aux(lf_t2d, nh, pb):
    rows, page = lf_t2d.shape
    n_phys = rows // nh
    return pl.pallas_call(
        _lfaux_kernel,
        grid=(n_phys // pb,),
        in_specs=[pl.BlockSpec((pb * nh, page), lambda i: (i, 0))],
        out_specs=pl.BlockSpec((pb, 2 * nh, page), lambda i: (i, 0, 0)),
        out_shape=jax.ShapeDtypeStruct((n_phys, 2 * nh, page), F32),
        compiler_params=_cparams(("parallel",)),
        name="lfaux",
    )(lf_t2d)


def _dec_kernel(pt_ref, qa_ref, kan_ref, van_ref, qb_ref, kbn_ref, vbn_ref, lfn_ref,
                kf_hbm, vf_hbm, ks_hbm, vs_hbm, aux_hbm,
                oa_ref, ob_ref,
                kf_buf, vf_buf, ks_buf, vs_buf, aux_buf, sem,
                qa_s, qb_s, new_pad, m_a, l_a, acc_a, run_f, acc_b, run_s,
                *, n_pages, page, chunk, scale):
    b = pl.program_id(0)
    c = pl.program_id(1)
    nb = pl.num_programs(0)
    nc = pl.num_programs(1)
    step = b * nc + c
    slot = step % 2
    nt, width = qa_ref.shape
    nh = width // HEAD_DIM
    tp = SUBLANES
    rows = nh * tp
    ck = chunk * page
    prow = page * nh
    caches = ((kf_hbm, kf_buf), (vf_hbm, vf_buf), (ks_hbm, ks_buf), (vs_hbm, vs_buf))

    def page_copies(step_idx, slot_idx):
        sb = step_idx // nc
        sc = step_idx % nc
        base = sb * n_pages + (n_pages - (sc + 1) * chunk)
        copies = []
        for i in range(chunk):
            pid = pt_ref[base + i]
            for ci, (src, dst) in enumerate(caches):
                copies.append(pltpu.make_async_copy(
                    src.at[pl.ds(pl.multiple_of(pid * prow, prow), prow)],
                    dst.at[slot_idx, pl.ds(i * prow, prow)], sem.at[slot_idx, ci]))
            copies.append(pltpu.make_async_copy(aux_hbm.at[pid], aux_buf.at[slot_idx, i], sem.at[slot_idx, 4]))
        return copies

    @pl.when(step == 0)
    def _():
        for cp in page_copies(step, slot):
            cp.start()
        new_pad[...] = jnp.zeros(new_pad.shape, F32)

    @pl.when(step + 1 < nb * nc)
    def _():
        for cp in page_copies(step + 1, 1 - slot):
            cp.start()

    def head(x, h):
        return x[h * tp:(h + 1) * tp]

    def per_head(fn):
        return jnp.concatenate([fn(h) for h in range(nh)], axis=0)

    def spread_heads(x8):
        return per_head(lambda h: jnp.broadcast_to(x8[h:h + 1, :], (tp, x8.shape[1])))

    def scores(q_s, key_fn):
        return per_head(lambda h: lax.dot_general(head(q_s[...], h).astype(BF16), key_fn(h), NT_DIMS,
                                                  preferred_element_type=F32))

    def weighted(p, val_fn):
        pb16 = p.astype(BF16)
        return per_head(lambda h: jnp.dot(head(pb16, h), val_fn(h), preferred_element_type=F32))

    @pl.when(c == 0)
    def _():
        def head_rows(q_ref):
            q = q_ref[...] * scale
            pad = jnp.zeros((tp - nt, HEAD_DIM), F32)
            return jnp.concatenate(
                [piece for h in range(nh) for piece in (q[:, h * HEAD_DIM:(h + 1) * HEAD_DIM], pad)], axis=0)

        qa_s[...] = head_rows(qa_ref)
        qb_s[...] = head_rows(qb_ref)
        tok = lax.broadcasted_iota(jnp.int32, (rows, page), 0) % tp
        key = lax.broadcasted_iota(jnp.int32, (rows, page), 1)
        m01 = _strict_lower_ones(page, BF16)

        def padded(ref, idx):
            new_pad[idx, 0:nt, :] = ref[...]
            return lambda h: new_pad[idx, :, h * HEAD_DIM:(h + 1) * HEAD_DIM].astype(BF16)

        lfn = lfn_ref[...]
        s = scores(qa_s, padded(kan_ref, 0))
        s = jnp.where(key <= tok, s, -jnp.inf) + spread_heads(_dot_split(lfn, m01, 3))
        m_new = jnp.max(s, axis=-1, keepdims=True)
        p = jnp.exp(s - m_new)
        m_a[...] = m_new
        l_a[...] = jnp.sum(p, axis=-1, keepdims=True)
        acc_a[...] = weighted(p, padded(van_ref, 1))
        run_f[...] = jnp.broadcast_to(jnp.sum(lfn, axis=-1, keepdims=True), run_f.shape)

        z = scores(qb_s, padded(kbn_ref, 2))
        log_keep, log_beta = _sb_scores(z)
        before = key < tok
        log_keep = jnp.where(before, log_keep, 0.0)
        w = jnp.where(before, jnp.exp(log_beta + _dot_split(log_keep, m01, 2)), 0.0)
        acc_b[...] = weighted(w, padded(vbn_ref, 3))
        run_s[...] = jnp.sum(log_keep, axis=-1, keepdims=True)

    for cp in page_copies(step, slot):
        cp.wait()

    def cached(buf):
        return lambda h: buf[slot, pl.ds(h, ck, stride=nh), :].astype(BF16)

    s = scores(qa_s, cached(kf_buf))
    run = run_f[...]
    bias = [None] * chunk
    for i in reversed(range(chunk)):
        bias[i] = aux_buf[slot, i, 0:nh, :] + run
        run = run + aux_buf[slot, i, nh:2 * nh, :]
    run_f[...] = run
    s = s + spread_heads(jnp.concatenate(bias, axis=1))
    m_old = m_a[...]
    m_new = jnp.maximum(m_old, jnp.max(s, axis=-1, keepdims=True))
    p = jnp.exp(s - m_new)
    alpha = jnp.exp(m_old - m_new)
    m_a[...] = m_new
    l_a[...] = alpha * l_a[...] + jnp.sum(p, axis=-1, keepdims=True)
    acc_a[...] = alpha * acc_a[...] + weighted(p, cached(vf_buf))

    z = scores(qb_s, cached(ks_buf))
    log_keep, log_beta = _sb_scores(z)
    later = _dot_split(log_keep, _strict_lower_ones(ck, BF16), 2) + run_s[...]
    w = jnp.exp(log_beta + later)
    acc_b[...] = acc_b[...] + weighted(w, cached(vs_buf))
    run_s[...] = run_s[...] + jnp.sum(log_keep, axis=-1, keepdims=True)

    @pl.when(c == nc - 1)
    def _():
        fa = acc_a[...] / l_a[...]
        fb = acc_b[...]
        for h in range(nh):
            oa_ref[:, h * HEAD_DIM:(h + 1) * HEAD_DIM] = head(fa, h)[0:nt]
            ob_ref[:, h * HEAD_DIM:(h + 1) * HEAD_DIM] = head(fb, h)[0:nt]


def _decode_attention(page_table, qa, kan, van, qb, kbn, vbn, lfn, kf, vf, ks, vs, aux, chunk):
    nb, nt, width = qa.shape
    nh = width // HEAD_DIM
    n_pages = page_table.shape[1]
    page = kf.shape[2]
    rows = nh * SUBLANES
    as_rows = lambda t: t.reshape(-1, HEAD_DIM)
    nc = n_pages // chunk
    tok = pl.BlockSpec((None, nt, width), lambda b, c, pt: (b, 0, 0))
    hbm = pl.BlockSpec(memory_space=pl.ANY)
    page_buf = pltpu.VMEM((2, chunk * page * nh, HEAD_DIM), F32)
    grid_spec = pltpu.PrefetchScalarGridSpec(
        num_scalar_prefetch=1,
        grid=(nb, nc),
        in_specs=[tok, tok, tok, tok, tok, tok,
                  pl.BlockSpec((None, nh, page), lambda b, c, pt: (b, 0, 0)),
                  hbm, hbm, hbm, hbm, hbm],
        out_specs=[tok, tok],
        scratch_shapes=[
            page_buf, page_buf, page_buf, page_buf,
            pltpu.VMEM((2, chunk, 2 * nh, page), F32),
            pltpu.SemaphoreType.DMA((2, 5)),
            pltpu.VMEM((rows, HEAD_DIM), F32),
            pltpu.VMEM((rows, HEAD_DIM), F32),
            pltpu.VMEM((4, page, width), F32),
            pltpu.VMEM((rows, 1), F32),
            pltpu.VMEM((rows, 1), F32),
            pltpu.VMEM((rows, HEAD_DIM), F32),
            pltpu.VMEM((nh, page), F32),
            pltpu.VMEM((rows, HEAD_DIM), F32),
            pltpu.VMEM((rows, 1), F32),
        ],
    )
    return pl.pallas_call(
        functools.partial(_dec_kernel, n_pages=n_pages, page=page, chunk=chunk, scale=HEAD_DIM ** -0.5),
        grid_spec=grid_spec,
        out_shape=[jax.ShapeDtypeStruct((nb, nt, width), F32)] * 2,
        compiler_params=_cparams(("arbitrary", "arbitrary")),
        name="dec",
    )(page_table.reshape(-1), qa, kan, van, qb, kbn, vbn, lfn,
      as_rows(kf), as_rows(vf), as_rows(ks), as_rows(vs), aux)


def _project(x2d, gain, w_main, w_r, wft, bf, width, tm_norm, tm):
    d = x2d.shape[1]
    h, lft = _norm(x2d, gain, wft, bf, tm_norm)
    groups = [_matmul(h, w_main, g, width, tm, width, "proj_main") for g in range(8)]
    r_a = _matmul(h, w_r, 0, d, tm, width, "proj_gate")
    r_b = _matmul(h, w_r, d // width, d, tm, width, "proj_gate")
    return groups, r_a, r_b, lft


def kernel(x_prompt, x_sample, cache_k_fox, cache_v_fox, cache_logf_fox, cache_k_sb, cache_v_sb, page_table,
           norm_gain, w_in, b_forget, w_branch_fox, w_branch_sb, w_out, final_norm_gain):
    depth = norm_gain.shape[0]
    assert depth == 1, "single-layer step"
    batch, seq, d = x_prompt.shape
    nb, nt, _ = x_sample.shape
    nh = b_forget.shape[1]
    width = nh * HEAD_DIM
    n_phys, page = cache_k_fox.shape[1], cache_k_fox.shape[2]

    w = w_in[0]
    w_main = w[:, :8 * width].astype(BF16)
    wft = w[:, 8 * width:8 * width + nh].T.astype(BF16)
    w_r = w[:, 8 * width + nh:].astype(BF16)
    bf = b_forget[0].reshape(nh, 1)
    gain = norm_gain[0].reshape(1, d)
    fgain = final_norm_gain.reshape(1, d)
    wa = w_branch_fox[0].astype(BF16)
    wb = w_branch_sb[0].astype(BF16)
    wo = w_out[0].astype(BF16)

    xp = x_prompt.reshape(batch * seq, d)
    (qa, ka, va, ga, qb, kb, vb, gb), ra, rb, lft_p = _project(xp, gain, w_main, w_r, wft, bf, width, 256, 512)
    sfx = _suffix_sums(lft_p, seq).reshape(nh, 1, batch * seq)
    oa = _fox_prompt(qa, ka, va, sfx, batch, seq, 256)
    ob = _sb_prompt(qb, kb, vb, batch, seq, 256)
    y_p = _merge(oa, ga, ob, gb, ra, rb, xp, wa, wb, wo, fgain, 256)

    xs = x_sample.reshape(nb * nt, d)
    (qa_s, ka_s, va_s, ga_s, qb_s, kb_s, vb_s, gb_s), ra_s, rb_s, lft_s = _project(
        xs, gain, w_main, w_r, wft, bf, width, nb * nt, nb * nt)
    lf_t = jnp.swapaxes(cache_logf_fox[0], 1, 2).reshape(n_phys * nh, page)
    aux = _logf_page_aux(lf_t, nh, 64)
    lfn = jnp.pad(jnp.transpose(lft_s.reshape(nh, nb, nt), (1, 0, 2)), ((0, 0), (0, 0), (0, page - nt)))
    tok3 = lambda t: t.reshape(nb, nt, width)
    oa_s, ob_s = _decode_attention(
        page_table, tok3(qa_s), tok3(ka_s), tok3(va_s), tok3(qb_s), tok3(kb_s), tok3(vb_s), lfn,
        cache_k_fox, cache_v_fox, cache_k_sb, cache_v_sb, aux, 4)
    y_s = _merge(oa_s.reshape(nb * nt, width), ga_s, ob_s.reshape(nb * nt, width), gb_s, ra_s, rb_s, xs,
                 wa, wb, wo, fgain, nb * nt)

    heads_p = lambda t: t.reshape(1, batch, seq, nh, HEAD_DIM)
    heads_s = lambda t: t.reshape(1, nb, nt, nh, HEAD_DIM)
    return (
        y_p.reshape(batch, seq, d), y_s.reshape(nb, nt, d),
        heads_p(ka), heads_p(va), lft_p.T.reshape(1, batch, seq, nh), heads_p(kb), heads_p(vb),
        heads_s(ka_s), heads_s(va_s), lft_s.T.reshape(1, nb, nt, nh), heads_s(kb_s), heads_s(vb_s),
    )
```

```python
import functools

import jax
import jax.numpy as jnp
from jax import lax
from jax.experimental import pallas as pl
from jax.experimental.pallas import tpu as pltpu

HEAD_DIM = 128
RMS_EPS = 1e-6
VMEM_LIMIT_BYTES = 56 * 1024 * 1024
SUBLANES = 8
F32 = jnp.float32
BF16 = jnp.bfloat16
NT_DIMS = (((1,), (1,)), ((), ()))


def _cparams(sem):
    return pltpu.CompilerParams(dimension_semantics=sem, vmem_limit_bytes=VMEM_LIMIT_BYTES)


def _log_sigmoid(u):
    return jnp.minimum(u, 0.0) - jnp.log1p(jnp.exp(-jnp.abs(u)))


def _strict_lower_ones(n, dtype):
    row = lax.broadcasted_iota(jnp.int32, (n, n), 0)
    col = lax.broadcasted_iota(jnp.int32, (n, n), 1)
    return jnp.where(row > col, 1.0, 0.0).astype(dtype)


def _dot_split(x, m01, terms):
    acc = None
    rem = x
    for i in range(terms):
        piece = rem.astype(BF16)
        part = jnp.dot(piece, m01, preferred_element_type=F32)
        acc = part if acc is None else acc + part
        if i + 1 < terms:
            rem = rem - piece.astype(F32)
    return acc


def _norm_kernel(x_ref, gain_ref, wft_ref, bf_ref, h_ref, lft_ref):
    x = x_ref[...]
    ms = jnp.mean(x * x, axis=-1, keepdims=True)
    hb = (x * lax.rsqrt(ms + RMS_EPS) * gain_ref[...]).astype(BF16)
    h_ref[...] = hb
    f = lax.dot_general(wft_ref[...], hb, NT_DIMS, preferred_element_type=F32)
    lft_ref[...] = _log_sigmoid(f + bf_ref[...])


def _norm(x2d, gain, wft, bf, tm):
    m, d = x2d.shape
    nh = wft.shape[0]
    return pl.pallas_call(
        _norm_kernel,
        grid=(m // tm,),
        in_specs=[
            pl.BlockSpec((tm, d), lambda i: (i, 0)),
            pl.BlockSpec((1, d), lambda i: (0, 0)),
            pl.BlockSpec((nh, d), lambda i: (0, 0)),
            pl.BlockSpec((nh, 1), lambda i: (0, 0)),
        ],
        out_specs=[
            pl.BlockSpec((tm, d), lambda i: (i, 0)),
            pl.BlockSpec((nh, tm), lambda i: (0, i)),
        ],
        out_shape=[
            jax.ShapeDtypeStruct((m, d), BF16),
            jax.ShapeDtypeStruct((nh, m), F32),
        ],
        compiler_params=_cparams(("parallel",)),
        name="norm",
    )(x2d, gain, wft, bf)


def _mm_kernel(h_ref, w_ref, o_ref):
    o_ref[...] = jnp.dot(h_ref[...], w_ref[...], preferred_element_type=F32)


def _matmul(h, w, col_block0, n, tm, tn, name):
    m, k = h.shape
    return pl.pallas_call(
        _mm_kernel,
        grid=(m // tm, n // tn),
        in_specs=[
            pl.BlockSpec((tm, k), lambda i, j: (i, 0)),
            pl.BlockSpec((k, tn), lambda i, j: (0, col_block0 + j)),
        ],
        out_specs=pl.BlockSpec((tm, tn), lambda i, j: (i, j)),
        out_shape=jax.ShapeDtypeStruct((m, n), F32),
        compiler_params=_cparams(("parallel", "parallel")),
        name=name,
    )(h, w)


def _sfx_kernel(lf_ref, o_ref, *, blk):
    t = lf_ref.shape[1]
    m01 = _strict_lower_ones(blk, BF16)
    carry = jnp.zeros((lf_ref.shape[0], 1), F32)
    for c in reversed(range(t // blk)):
        x = lf_ref[:, c * blk:(c + 1) * blk]
        o_ref[:, c * blk:(c + 1) * blk] = _dot_split(x, m01, 3) + carry
        carry = carry + jnp.sum(x, axis=-1, keepdims=True)


def _suffix_sums(lft, seq):
    nh, m = lft.shape
    return pl.pallas_call(
        functools.partial(_sfx_kernel, blk=256),
        grid=(m // seq,),
        in_specs=[pl.BlockSpec((nh, seq), lambda b: (0, b))],
        out_specs=pl.BlockSpec((nh, seq), lambda b: (0, b)),
        out_shape=jax.ShapeDtypeStruct((nh, m), F32),
        compiler_params=_cparams(("parallel",)),
        name="sfx",
    )(lft)


def _head_cols(g):
    return slice(g * HEAD_DIM, (g + 1) * HEAD_DIM)


def _fox_kernel(q_ref, k_ref, v_ref, sfx_ref, o_ref, *, tq, heads, scale):
    qi = pl.program_id(2)
    qs = [(q_ref[:, _head_cols(g)] * scale).astype(BF16) for g in range(heads)]

    def step(kb, carry, diagonal):
        off = pl.multiple_of(kb * tq, tq)
        if diagonal:
            row = lax.broadcasted_iota(jnp.int32, (tq, tq), 0)
            col = lax.broadcasted_iota(jnp.int32, (tq, tq), 1)
            visible = col <= row
        out = []
        for g in range(heads):
            m_run, l_run, acc = carry[g]
            k = k_ref[pl.ds(off, tq), _head_cols(g)].astype(BF16)
            v = v_ref[pl.ds(off, tq), _head_cols(g)].astype(BF16)
            s = lax.dot_general(qs[g], k, NT_DIMS, preferred_element_type=F32)
            s = s + sfx_ref[g, :, pl.ds(off, tq)]
            if diagonal:
                s = jnp.where(visible, s, -jnp.inf)
            m_new = jnp.maximum(m_run, jnp.max(s, axis=-1, keepdims=True))
            p = jnp.exp(s - m_new)
            alpha = jnp.exp(m_run - m_new)
            l_new = alpha * l_run + jnp.sum(p, axis=-1, keepdims=True)
            acc = alpha * acc + jnp.dot(p.astype(BF16), v, preferred_element_type=F32)
            out.append((m_new, l_new, acc))
        return tuple(out)

    init = tuple((jnp.full((tq, 1), -jnp.inf, F32), jnp.zeros((tq, 1), F32), jnp.zeros((tq, HEAD_DIM), F32))
                 for _ in range(heads))
    carry = lax.fori_loop(0, qi, lambda kb, c: step(kb, c, False), init)
    carry = step(qi, carry, True)
    for g in range(heads):
        o_ref[:, _head_cols(g)] = carry[g][2] / carry[g][1]


def _fox_prompt(q, k, v, sfx3, batch, seq, tq, heads):
    m, width = q.shape
    nh = width // HEAD_DIM
    nq = seq // tq
    cols = heads * HEAD_DIM
    return pl.pallas_call(
        functools.partial(_fox_kernel, tq=tq, heads=heads, scale=HEAD_DIM ** -0.5),
        grid=(batch, nh // heads, nq),
        in_specs=[
            pl.BlockSpec((tq, cols), lambda b, h, i: (b * nq + i, h)),
            pl.BlockSpec((seq, cols), lambda b, h, i: (b, h)),
            pl.BlockSpec((seq, cols), lambda b, h, i: (b, h)),
            pl.BlockSpec((heads, 1, seq), lambda b, h, i: (h, 0, b)),
        ],
        out_specs=pl.BlockSpec((tq, cols), lambda b, h, i: (b * nq + i, h)),
        out_shape=jax.ShapeDtypeStruct((m, width), F32),
        compiler_params=_cparams(("parallel", "parallel", "arbitrary")),
        name="fox_p",
    )(q, k, v, sfx3)


def _sb_scores(z):
    l1 = jnp.log1p(jnp.exp(-jnp.abs(z)))
    return -(jnp.maximum(z, 0.0) + l1), jnp.minimum(z, 0.0) - l1


SB_DEAD_LOG = -120.0
SB_CUMSUM_BLOCK = 512


def _sb_kernel(q_ref, k_ref, v_ref, o_ref, *, tq, heads, scale):
    qi = pl.program_id(2)
    qs = [(q_ref[:, _head_cols(g)] * scale).astype(BF16) for g in range(heads)]
    m01 = _strict_lower_ones(tq, BF16)

    def step(kb, carry, diagonal):
        off = pl.multiple_of(kb * tq, tq)
        if diagonal:
            row = lax.broadcasted_iota(jnp.int32, (tq, tq), 0)
            col = lax.broadcasted_iota(jnp.int32, (tq, tq), 1)
            before = col < row
        out = []
        for g in range(heads):
            run, acc = carry[g]
            k = k_ref[pl.ds(off, tq), _head_cols(g)].astype(BF16)
            v = v_ref[pl.ds(off, tq), _head_cols(g)].astype(BF16)
            z = lax.dot_general(qs[g], k, NT_DIMS, preferred_element_type=F32)
            log_keep, log_beta = _sb_scores(z)
            if diagonal:
                log_keep = jnp.where(before, log_keep, 0.0)
            later = _dot_split(log_keep, m01, 2) + run
            w = jnp.exp(log_beta + later)
            if diagonal:
                w = jnp.where(before, w, 0.0)
            acc = acc + jnp.dot(w.astype(BF16), v, preferred_element_type=F32)
            run = run + jnp.sum(log_keep, axis=-1, keepdims=True)
            out.append((run, acc))
        return tuple(out)

    def alive(carry):
        top = carry[0][0]
        for g in range(1, heads):
            top = jnp.maximum(top, carry[g][0])
        return jnp.max(top) > SB_DEAD_LOG

    init = tuple((jnp.zeros((tq, 1), F32), jnp.zeros((tq, HEAD_DIM), F32)) for _ in range(heads))
    carry = step(qi, init, True)

    def body(state):
        i, _, carry = state
        carry = step(qi - 1 - i, carry, False)
        return i + 1, alive(carry), carry

    _, _, carry = lax.while_loop(lambda st: jnp.logical_and(st[0] < qi, st[1]), body,
                                 (jnp.int32(0), alive(carry), carry))
    for g in range(heads):
        o_ref[:, _head_cols(g)] = carry[g][1]


def _sb_prompt(q, k, v, batch, seq, tq, heads):
    m, width = q.shape
    nh = width // HEAD_DIM
    nq = seq // tq
    cols = heads * HEAD_DIM
    return pl.pallas_call(
        functools.partial(_sb_kernel, tq=tq, heads=heads, scale=HEAD_DIM ** -0.5),
        grid=(batch, nh // heads, nq),
        in_specs=[
            pl.BlockSpec((tq, cols), lambda b, h, i: (b * nq + i, h)),
            pl.BlockSpec((seq, cols), lambda b, h, i: (b, h)),
            pl.BlockSpec((seq, cols), lambda b, h, i: (b, h)),
        ],
        out_specs=pl.BlockSpec((tq, cols), lambda b, h, i: (b * nq + i, h)),
        out_shape=jax.ShapeDtypeStruct((m, width), F32),
        compiler_params=_cparams(("parallel", "parallel", "arbitrary")),
        name="sb_p",
    )(q, k, v)


def _merge_kernel(oa_ref, ga_ref, ob_ref, gb_ref, ra_ref, rb_ref, x_ref, wa_ref, wb_ref, wo_ref, gain_ref, y_ref):
    ga = ga_ref[...]
    gb = gb_ref[...]
    a = (oa_ref[...] * (ga * jax.nn.sigmoid(ga))).astype(BF16)
    b = (ob_ref[...] * (gb * jax.nn.sigmoid(gb))).astype(BF16)
    ua = jnp.dot(a, wa_ref[...], preferred_element_type=F32)
    ub = jnp.dot(b, wb_ref[...], preferred_element_type=F32)
    mix = (jax.nn.sigmoid(ra_ref[...]) * ua + jax.nn.sigmoid(rb_ref[...]) * ub).astype(BF16)
    res = x_ref[...] + jnp.dot(mix, wo_ref[...], preferred_element_type=F32)
    ms = jnp.mean(res * res, axis=-1, keepdims=True)
    y_ref[...] = res * lax.rsqrt(ms + RMS_EPS) * gain_ref[...]


def _merge(oa, ga, ob, gb, ra, rb, x2d, wa, wb, wo, gain, tm):
    m, d = x2d.shape
    width = oa.shape[1]
    row = lambda w: pl.BlockSpec((tm, w), lambda i: (i, 0))
    const = lambda s: pl.BlockSpec(s, lambda i: (0, 0), pipeline_mode=pl.Buffered(1))
    return pl.pallas_call(
        _merge_kernel,
        grid=(m // tm,),
        in_specs=[row(width), row(width), row(width), row(width), row(d), row(d), row(d),
                  const((width, d)), const((width, d)), const((d, d)), const((1, d))],
        out_specs=row(d),
        out_shape=jax.ShapeDtypeStruct((m, d), F32),
        compiler_params=_cparams(("parallel",)),
        name="merge",
    )(oa, ga, ob, gb, ra, rb, x2d, wa, wb, wo, gain)


def _lfaux_kernel(x_ref, o_ref):
    pb = o_ref.shape[0]
    nh = x_ref.shape[0] // pb
    n = x_ref.shape[1]
    x = x_ref[...]
    sfx = _dot_split(x, _strict_lower_ones(n, BF16), 3)
    tot = _dot_split(x, jnp.ones((n, n), BF16), 3)
    o_ref[:, 0:nh, :] = sfx.reshape(pb, nh, n)
    o_ref[:, nh:2 * nh, :] = tot.reshape(pb, nh, n)


def _logf_page_aux(lf_t2d, nh, pb):
    rows, page = lf_t2d.shape
    n_phys = rows // nh
    return pl.pallas_call(
        _lfaux_kernel,
        grid=(n_phys // pb,),
        in_specs=[pl.BlockSpec((pb * nh, page), lambda i: (i, 0))],
        out_specs=pl.BlockSpec((pb, 2 * nh, page), lambda i: (i, 0, 0)),
        out_shape=jax.ShapeDtypeStruct((n_phys, 2 * nh, page), F32),
        compiler_params=_cparams(("parallel",)),
        name="lfaux",
    )(lf_t2d)


def _dec_kernel(pt_ref, qa_ref, kan_ref, van_ref, qb_ref, kbn_ref, vbn_ref, lfn_ref,
                kf_hbm, vf_hbm, ks_hbm, vs_hbm, aux_hbm,
                oa_ref, ob_ref,
                kf_buf, vf_buf, ks_buf, vs_buf, aux_buf, sem,
                qa_s, qb_s, new_pad, m_a, l_a, acc_a, run_f, acc_b, run_s, sb_pending,
                *, n_pages, page, chunk, scale):
    b = pl.program_id(0)
    c = pl.program_id(1)
    nb = pl.num_programs(0)
    nc = pl.num_programs(1)
    step = b * nc + c
    slot = step % 2
    nt, width = qa_ref.shape
    nh = width // HEAD_DIM
    tp = SUBLANES
    rows = nh * tp
    ck = chunk * page
    prow = page * nh
    fox_caches = ((kf_hbm, kf_buf, 0), (vf_hbm, vf_buf, 1))
    sb_caches = ((ks_hbm, ks_buf, 2), (vs_hbm, vs_buf, 3))

    def page_copies(step_idx, slot_idx, caches, with_aux):
        sb = step_idx // nc
        sc = step_idx % nc
        base = sb * n_pages + (n_pages - (sc + 1) * chunk)
        copies = []
        for i in range(chunk):
            pid = pt_ref[base + i]
            for src, dst, ci in caches:
                copies.append(pltpu.make_async_copy(
                    src.at[pl.ds(pl.multiple_of(pid * prow, prow), prow)],
                    dst.at[slot_idx, pl.ds(i * prow, prow)], sem.at[slot_idx, ci]))
            if with_aux:
                copies.append(pltpu.make_async_copy(aux_hbm.at[pid], aux_buf.at[slot_idx, i], sem.at[slot_idx, 4]))
        return copies

    def fox_copies(step_idx, slot_idx):
        return page_copies(step_idx, slot_idx, fox_caches, True)

    def sb_copies(step_idx, slot_idx):
        return page_copies(step_idx, slot_idx, sb_caches, False)

    @pl.when(step == 0)
    def _():
        for cp in fox_copies(step, slot) + sb_copies(step, slot):
            cp.start()
        sb_pending[0] = 1
        new_pad[...] = jnp.zeros(new_pad.shape, F32)

    @pl.when(step + 1 < nb * nc)
    def _():
        for cp in fox_copies(step + 1, 1 - slot):
            cp.start()

    def head(x, h):
        return x[h * tp:(h + 1) * tp]

    def per_head(fn):
        return jnp.concatenate([fn(h) for h in range(nh)], axis=0)

    def spread_heads(x8):
        return per_head(lambda h: jnp.broadcast_to(x8[h:h + 1, :], (tp, x8.shape[1])))

    def scores(q_s, key_fn):
        return per_head(lambda h: lax.dot_general(head(q_s[...], h).astype(BF16), key_fn(h), NT_DIMS,
                                                  preferred_element_type=F32))

    def weighted(p, val_fn):
        pb16 = p.astype(BF16)
        return per_head(lambda h: jnp.dot(head(pb16, h), val_fn(h), preferred_element_type=F32))

    @pl.when(c == 0)
    def _():
        def head_rows(q_ref):
            q = q_ref[...] * scale
            pad = jnp.zeros((tp - nt, HEAD_DIM), F32)
            return jnp.concatenate(
                [piece for h in range(nh) for piece in (q[:, h * HEAD_DIM:(h + 1) * HEAD_DIM], pad)], axis=0)

        qa_s[...] = head_rows(qa_ref)
        qb_s[...] = head_rows(qb_ref)
        tok = lax.broadcasted_iota(jnp.int32, (rows, page), 0) % tp
        key = lax.broadcasted_iota(jnp.int32, (rows, page), 1)
        m01 = _strict_lower_ones(page, BF16)

        def padded(ref, idx):
            new_pad[idx, 0:nt, :] = ref[...]
            return lambda h: new_pad[idx, :, h * HEAD_DIM:(h + 1) * HEAD_DIM].astype(BF16)

        lfn = lfn_ref[...]
        s = scores(qa_s, padded(kan_ref, 0))
        s = jnp.where(key <= tok, s, -jnp.inf) + spread_heads(_dot_split(lfn, m01, 3))
        m_new = jnp.max(s, axis=-1, keepdims=True)
        p = jnp.exp(s - m_new)
        m_a[...] = m_new
        l_a[...] = jnp.sum(p, axis=-1, keepdims=True)
        acc_a[...] = weighted(p, padded(van_ref, 1))
        run_f[...] = jnp.broadcast_to(jnp.sum(lfn, axis=-1, keepdims=True), run_f.shape)

        z = scores(qb_s, padded(kbn_ref, 2))
        log_keep, log_beta = _sb_scores(z)
        before = key < tok
        log_keep = jnp.where(before, log_keep, 0.0)
        w = jnp.where(before, jnp.exp(log_beta + _dot_split(log_keep, m01, 2)), 0.0)
        acc_b[...] = weighted(w, padded(vbn_ref, 3))
        run_s[...] = jnp.sum(log_keep, axis=-1, keepdims=True)

    for cp in fox_copies(step, slot):
        cp.wait()

    def cached(buf):
        return lambda h: buf[slot, pl.ds(h, ck, stride=nh), :].astype(BF16)

    s = scores(qa_s, cached(kf_buf))
    run = run_f[...]
    bias = [None] * chunk
    for i in reversed(range(chunk)):
        bias[i] = aux_buf[slot, i, 0:nh, :] + run
        run = run + aux_buf[slot, i, nh:2 * nh, :]
    run_f[...] = run
    s = s + spread_heads(jnp.concatenate(bias, axis=1))
    m_old = m_a[...]
    m_new = jnp.maximum(m_old, jnp.max(s, axis=-1, keepdims=True))
    p = jnp.exp(s - m_new)
    alpha = jnp.exp(m_old - m_new)
    m_a[...] = m_new
    l_a[...] = alpha * l_a[...] + jnp.sum(p, axis=-1, keepdims=True)
    acc_a[...] = alpha * acc_a[...] + weighted(p, cached(vf_buf))

    sb_live = sb_pending[0] == 1

    @pl.when(sb_live)
    def _():
        for cp in sb_copies(step, slot):
            cp.wait()
        z = scores(qb_s, cached(ks_buf))
        log_keep, log_beta = _sb_scores(z)
        sub = min(ck, SB_CUMSUM_BLOCK)
        m01 = _strict_lower_ones(sub, BF16)
        run = run_s[...]
        w = [None] * (ck // sub)
        for j in reversed(range(ck // sub)):
            cols = slice(j * sub, (j + 1) * sub)
            w[j] = jnp.exp(log_beta[:, cols] + _dot_split(log_keep[:, cols], m01, 2) + run)
            run = run + jnp.sum(log_keep[:, cols], axis=-1, keepdims=True)
        acc_b[...] = acc_b[...] + weighted(jnp.concatenate(w, axis=1), cached(vs_buf))
        run_s[...] = run

    still_alive = jnp.logical_and(sb_live, jnp.max(run_s[...]) > SB_DEAD_LOG)
    fetch_next = jnp.logical_and(step + 1 < nb * nc, jnp.logical_or(c == nc - 1, still_alive))
    sb_pending[0] = fetch_next.astype(jnp.int32)

    @pl.when(fetch_next)
    def _():
        for cp in sb_copies(step + 1, 1 - slot):
            cp.start()

    @pl.when(c == nc - 1)
    def _():
        fa = acc_a[...] / l_a[...]
        fb = acc_b[...]
        for h in range(nh):
            oa_ref[:, h * HEAD_DIM:(h + 1) * HEAD_DIM] = head(fa, h)[0:nt]
            ob_ref[:, h * HEAD_DIM:(h + 1) * HEAD_DIM] = head(fb, h)[0:nt]


def _decode_attention(page_table, qa, kan, van, qb, kbn, vbn, lfn, kf, vf, ks, vs, aux, chunk):
    nb, nt, width = qa.shape
    nh = width // HEAD_DIM
    n_pages = page_table.shape[1]
    page = kf.shape[2]
    rows = nh * SUBLANES
    as_rows = lambda t: t.reshape(-1, HEAD_DIM)
    nc = n_pages // chunk
    tok = pl.BlockSpec((None, nt, width), lambda b, c, pt: (b, 0, 0))
    hbm = pl.BlockSpec(memory_space=pl.ANY)
    page_buf = pltpu.VMEM((2, chunk * page * nh, HEAD_DIM), F32)
    grid_spec = pltpu.PrefetchScalarGridSpec(
        num_scalar_prefetch=1,
        grid=(nb, nc),
        in_specs=[tok, tok, tok, tok, tok, tok,
                  pl.BlockSpec((None, nh, page), lambda b, c, pt: (b, 0, 0)),
                  hbm, hbm, hbm, hbm, hbm],
        out_specs=[tok, tok],
        scratch_shapes=[
            page_buf, page_buf, page_buf, page_buf,
            pltpu.VMEM((2, chunk, 2 * nh, page), F32),
            pltpu.SemaphoreType.DMA((2, 5)),
            pltpu.VMEM((rows, HEAD_DIM), F32),
            pltpu.VMEM((rows, HEAD_DIM), F32),
            pltpu.VMEM((4, page, width), F32),
            pltpu.VMEM((rows, 1), F32),
            pltpu.VMEM((rows, 1), F32),
            pltpu.VMEM((rows, HEAD_DIM), F32),
            pltpu.VMEM((nh, page), F32),
            pltpu.VMEM((rows, HEAD_DIM), F32),
            pltpu.VMEM((rows, 1), F32),
            pltpu.SMEM((1,), jnp.int32),
        ],
    )
    return pl.pallas_call(
        functools.partial(_dec_kernel, n_pages=n_pages, page=page, chunk=chunk, scale=HEAD_DIM ** -0.5),
        grid_spec=grid_spec,
        out_shape=[jax.ShapeDtypeStruct((nb, nt, width), F32)] * 2,
        compiler_params=_cparams(("arbitrary", "arbitrary")),
        name="dec",
    )(page_table.reshape(-1), qa, kan, van, qb, kbn, vbn, lfn,
      as_rows(kf), as_rows(vf), as_rows(ks), as_rows(vs), aux)


def _project(x2d, gain, w_main, w_r, wft, bf, width, tm_norm, tm):
    d = x2d.shape[1]
    h, lft = _norm(x2d, gain, wft, bf, tm_norm)
    groups = [_matmul(h, w_main, g, width, tm, width, "proj_main") for g in range(8)]
    r_a = _matmul(h, w_r, 0, d, tm, width, "proj_gate")
    r_b = _matmul(h, w_r, d // width, d, tm, width, "proj_gate")
    return groups, r_a, r_b, lft


def kernel(x_prompt, x_sample, cache_k_fox, cache_v_fox, cache_logf_fox, cache_k_sb, cache_v_sb, page_table,
           norm_gain, w_in, b_forget, w_branch_fox, w_branch_sb, w_out, final_norm_gain):
    depth = norm_gain.shape[0]
    assert depth == 1, "single-layer step"
    batch, seq, d = x_prompt.shape
    nb, nt, _ = x_sample.shape
    nh = b_forget.shape[1]
    width = nh * HEAD_DIM
    n_phys, page = cache_k_fox.shape[1], cache_k_fox.shape[2]

    w = w_in[0]
    w_main = w[:, :8 * width].astype(BF16)
    wft = w[:, 8 * width:8 * width + nh].T.astype(BF16)
    w_r = w[:, 8 * width + nh:].astype(BF16)
    bf = b_forget[0].reshape(nh, 1)
    gain = norm_gain[0].reshape(1, d)
    fgain = final_norm_gain.reshape(1, d)
    wa = w_branch_fox[0].astype(BF16)
    wb = w_branch_sb[0].astype(BF16)
    wo = w_out[0].astype(BF16)

    xp = x_prompt.reshape(batch * seq, d)
    (qa, ka, va, ga, qb, kb, vb, gb), ra, rb, lft_p = _project(xp, gain, w_main, w_r, wft, bf, width, 256, 512)
    sfx = _suffix_sums(lft_p, seq).reshape(nh, 1, batch * seq)
    oa = _fox_prompt(qa, ka, va, sfx, batch, seq, 512, 2)
    ob = _sb_prompt(qb, kb, vb, batch, seq, 512, 2)
    y_p = _merge(oa, ga, ob, gb, ra, rb, xp, wa, wb, wo, fgain, 256)

    xs = x_sample.reshape(nb * nt, d)
    (qa_s, ka_s, va_s, ga_s, qb_s, kb_s, vb_s, gb_s), ra_s, rb_s, lft_s = _project(
        xs, gain, w_main, w_r, wft, bf, width, nb * nt, nb * nt)
    lf_t = jnp.swapaxes(cache_logf_fox[0], 1, 2).reshape(n_phys * nh, page)
    aux = _logf_page_aux(lf_t, nh, 64)
    lfn = jnp.pad(jnp.transpose(lft_s.reshape(nh, nb, nt), (1, 0, 2)), ((0, 0), (0, 0), (0, page - nt)))
    tok3 = lambda t: t.reshape(nb, nt, width)
    oa_s, ob_s = _decode_attention(
        page_table, tok3(qa_s), tok3(ka_s), tok3(va_s), tok3(qb_s), tok3(kb_s), tok3(vb_s), lfn,
        cache_k_fox, cache_v_fox, cache_k_sb, cache_v_sb, aux, 8)
    y_s = _merge(oa_s.reshape(nb * nt, width), ga_s, ob_s.reshape(nb * nt, width), gb_s, ra_s, rb_s, xs,
                 wa, wb, wo, fgain, nb * nt)

    heads_p = lambda t: t.reshape(1, batch, seq, nh, HEAD_DIM)
    heads_s = lambda t: t.reshape(1, nb, nt, nh, HEAD_DIM)
    return (
        y_p.reshape(batch, seq, d), y_s.reshape(nb, nt, d),
        heads_p(ka), heads_p(va), lft_p.T.reshape(1, batch, seq, nh), heads_p(kb), heads_p(vb),
        heads_s(ka_s), heads_s(va_s), lft_s.T.reshape(1, nb, nt, nh), heads_s(kb_s), heads_s(vb_s),
    )
```

```python
import functools

import jax
import jax.numpy as jnp
from jax import lax
from jax.experimental import pallas as pl
from jax.experimental.pallas import tpu as pltpu

HEAD_DIM = 128
RMS_EPS = 1e-6
VMEM_LIMIT_BYTES = 56 * 1024 * 1024
SUBLANES = 8
F32 = jnp.float32
BF16 = jnp.bfloat16
NT_DIMS = (((1,), (1,)), ((), ()))


def _cparams(sem):
    return pltpu.CompilerParams(dimension_semantics=sem, vmem_limit_bytes=VMEM_LIMIT_BYTES)


def _log_sigmoid(u):
    return jnp.minimum(u, 0.0) - jnp.log1p(jnp.exp(-jnp.abs(u)))


def _strict_lower_ones(n, dtype):
    row = lax.broadcasted_iota(jnp.int32, (n, n), 0)
    col = lax.broadcasted_iota(jnp.int32, (n, n), 1)
    return jnp.where(row > col, 1.0, 0.0).astype(dtype)


def _dot_split(x, m01, terms):
    acc = None
    rem = x
    for i in range(terms):
        piece = rem.astype(BF16)
        part = jnp.dot(piece, m01, preferred_element_type=F32)
        acc = part if acc is None else acc + part
        if i + 1 < terms:
            rem = rem - piece.astype(F32)
    return acc


def _norm_kernel(x_ref, gain_ref, wft_ref, bf_ref, h_ref, lft_ref):
    x = x_ref[...]
    ms = jnp.mean(x * x, axis=-1, keepdims=True)
    hb = (x * lax.rsqrt(ms + RMS_EPS) * gain_ref[...]).astype(BF16)
    h_ref[...] = hb
    f = lax.dot_general(wft_ref[...], hb, NT_DIMS, preferred_element_type=F32)
    lft_ref[...] = _log_sigmoid(f + bf_ref[...])


def _norm(x2d, gain, wft, bf, tm):
    m, d = x2d.shape
    nh = wft.shape[0]
    return pl.pallas_call(
        _norm_kernel,
        grid=(m // tm,),
        in_specs=[
            pl.BlockSpec((tm, d), lambda i: (i, 0)),
            pl.BlockSpec((1, d), lambda i: (0, 0)),
            pl.BlockSpec((nh, d), lambda i: (0, 0)),
            pl.BlockSpec((nh, 1), lambda i: (0, 0)),
        ],
        out_specs=[
            pl.BlockSpec((tm, d), lambda i: (i, 0)),
            pl.BlockSpec((nh, tm), lambda i: (0, i)),
        ],
        out_shape=[
            jax.ShapeDtypeStruct((m, d), BF16),
            jax.ShapeDtypeStruct((nh, m), F32),
        ],
        compiler_params=_cparams(("parallel",)),
        name="norm",
    )(x2d, gain, wft, bf)


def _mm_kernel(h_ref, w_ref, o_ref):
    o_ref[...] = jnp.dot(h_ref[...], w_ref[...], preferred_element_type=F32)


def _matmul(h, w, col_block0, n, tm, tn, name):
    m, k = h.shape
    return pl.pallas_call(
        _mm_kernel,
        grid=(m // tm, n // tn),
        in_specs=[
            pl.BlockSpec((tm, k), lambda i, j: (i, 0)),
            pl.BlockSpec((k, tn), lambda i, j: (0, col_block0 + j)),
        ],
        out_specs=pl.BlockSpec((tm, tn), lambda i, j: (i, j)),
        out_shape=jax.ShapeDtypeStruct((m, n), F32),
        compiler_params=_cparams(("parallel", "parallel")),
        name=name,
    )(h, w)


def _sfx_kernel(lf_ref, o_ref, *, blk):
    t = lf_ref.shape[1]
    m01 = _strict_lower_ones(blk, BF16)
    carry = jnp.zeros((lf_ref.shape[0], 1), F32)
    for c in reversed(range(t // blk)):
        x = lf_ref[:, c * blk:(c + 1) * blk]
        o_ref[:, c * blk:(c + 1) * blk] = _dot_split(x, m01, 3) + carry
        carry = carry + jnp.sum(x, axis=-1, keepdims=True)


def _suffix_sums(lft, seq):
    nh, m = lft.shape
    return pl.pallas_call(
        functools.partial(_sfx_kernel, blk=256),
        grid=(m // seq,),
        in_specs=[pl.BlockSpec((nh, seq), lambda b: (0, b))],
        out_specs=pl.BlockSpec((nh, seq), lambda b: (0, b)),
        out_shape=jax.ShapeDtypeStruct((nh, m), F32),
        compiler_params=_cparams(("parallel",)),
        name="sfx",
    )(lft)


def _head_cols(g):
    return slice(g * HEAD_DIM, (g + 1) * HEAD_DIM)


def _fox_kernel(q_ref, k_ref, v_ref, sfx_ref, o_ref, *, tq, heads, scale):
    qi = pl.program_id(2)
    qs = [(q_ref[:, _head_cols(g)] * scale).astype(BF16) for g in range(heads)]

    def step(kb, carry, diagonal):
        off = pl.multiple_of(kb * tq, tq)
        if diagonal:
            row = lax.broadcasted_iota(jnp.int32, (tq, tq), 0)
            col = lax.broadcasted_iota(jnp.int32, (tq, tq), 1)
            visible = col <= row
        out = []
        for g in range(heads):
            m_run, l_run, acc = carry[g]
            k = k_ref[pl.ds(off, tq), _head_cols(g)].astype(BF16)
            v = v_ref[pl.ds(off, tq), _head_cols(g)].astype(BF16)
            s = lax.dot_general(qs[g], k, NT_DIMS, preferred_element_type=F32)
            s = s + sfx_ref[g, :, pl.ds(off, tq)]
            if diagonal:
                s = jnp.where(visible, s, -jnp.inf)
            m_new = jnp.maximum(m_run, jnp.max(s, axis=-1, keepdims=True))
            p = jnp.exp(s - m_new)
            alpha = jnp.exp(m_run - m_new)
            l_new = alpha * l_run + jnp.sum(p, axis=-1, keepdims=True)
            acc = alpha * acc + jnp.dot(p.astype(BF16), v, preferred_element_type=F32)
            out.append((m_new, l_new, acc))
        return tuple(out)

    init = tuple((jnp.full((tq, 1), -jnp.inf, F32), jnp.zeros((tq, 1), F32), jnp.zeros((tq, HEAD_DIM), F32))
                 for _ in range(heads))
    carry = lax.fori_loop(0, qi, lambda kb, c: step(kb, c, False), init)
    carry = step(qi, carry, True)
    for g in range(heads):
        o_ref[:, _head_cols(g)] = carry[g][2] / carry[g][1]


def _fox_prompt(q, k, v, sfx3, batch, seq, tq, heads):
    m, width = q.shape
    nh = width // HEAD_DIM
    nq = seq // tq
    cols = heads * HEAD_DIM
    return pl.pallas_call(
        functools.partial(_fox_kernel, tq=tq, heads=heads, scale=HEAD_DIM ** -0.5),
        grid=(batch, nh // heads, nq),
        in_specs=[
            pl.BlockSpec((tq, cols), lambda b, h, i: (b * nq + i, h)),
            pl.BlockSpec((seq, cols), lambda b, h, i: (b, h)),
            pl.BlockSpec((seq, cols), lambda b, h, i: (b, h)),
            pl.BlockSpec((heads, 1, seq), lambda b, h, i: (h, 0, b)),
        ],
        out_specs=pl.BlockSpec((tq, cols), lambda b, h, i: (b * nq + i, h)),
        out_shape=jax.ShapeDtypeStruct((m, width), F32),
        compiler_params=_cparams(("parallel", "parallel", "arbitrary")),
        name="fox_p",
    )(q, k, v, sfx3)


def _sb_scores(z):
    l1 = jnp.log1p(jnp.exp(-jnp.abs(z)))
    return -(jnp.maximum(z, 0.0) + l1), jnp.minimum(z, 0.0) - l1


SB_DEAD_LOG = -120.0
SB_CUMSUM_BLOCK = 512


def _sb_kernel(q_ref, k_ref, v_ref, o_ref, *, tq, heads, scale):
    qi = pl.program_id(2)
    qs = [(q_ref[:, _head_cols(g)] * scale).astype(BF16) for g in range(heads)]
    m01 = _strict_lower_ones(tq, BF16)

    def step(kb, carry, diagonal):
        off = pl.multiple_of(kb * tq, tq)
        if diagonal:
            row = lax.broadcasted_iota(jnp.int32, (tq, tq), 0)
            col = lax.broadcasted_iota(jnp.int32, (tq, tq), 1)
            before = col < row
        out = []
        for g in range(heads):
            run, acc = carry[g]
            k = k_ref[pl.ds(off, tq), _head_cols(g)].astype(BF16)
            v = v_ref[pl.ds(off, tq), _head_cols(g)].astype(BF16)
            z = lax.dot_general(qs[g], k, NT_DIMS, preferred_element_type=F32)
            log_keep, log_beta = _sb_scores(z)
            if diagonal:
                log_keep = jnp.where(before, log_keep, 0.0)
            later = _dot_split(log_keep, m01, 2) + run
            w = jnp.exp(log_beta + later)
            if diagonal:
                w = jnp.where(before, w, 0.0)
            acc = acc + jnp.dot(w.astype(BF16), v, preferred_element_type=F32)
            run = run + jnp.sum(log_keep, axis=-1, keepdims=True)
            out.append((run, acc))
        return tuple(out)

    def alive(carry):
        top = carry[0][0]
        for g in range(1, heads):
            top = jnp.maximum(top, carry[g][0])
        return jnp.max(top) > SB_DEAD_LOG

    init = tuple((jnp.zeros((tq, 1), F32), jnp.zeros((tq, HEAD_DIM), F32)) for _ in range(heads))
    carry = step(qi, init, True)

    def body(state):
        i, _, carry = state
        carry = step(qi - 1 - i, carry, False)
        return i + 1, alive(carry), carry

    _, _, carry = lax.while_loop(lambda st: jnp.logical_and(st[0] < qi, st[1]), body,
                                 (jnp.int32(0), alive(carry), carry))
    for g in range(heads):
        o_ref[:, _head_cols(g)] = carry[g][1]


def _sb_prompt(q, k, v, batch, seq, tq, heads):
    m, width = q.shape
    nh = width // HEAD_DIM
    nq = seq // tq
    cols = heads * HEAD_DIM
    return pl.pallas_call(
        functools.partial(_sb_kernel, tq=tq, heads=heads, scale=HEAD_DIM ** -0.5),
        grid=(batch, nh // heads, nq),
        in_specs=[
            pl.BlockSpec((tq, cols), lambda b, h, i: (b * nq + i, h)),
            pl.BlockSpec((seq, cols), lambda b, h, i: (b, h)),
            pl.BlockSpec((seq, cols), lambda b, h, i: (b, h)),
        ],
        out_specs=pl.BlockSpec((tq, cols), lambda b, h, i: (b * nq + i, h)),
        out_shape=jax.ShapeDtypeStruct((m, width), F32),
        compiler_params=_cparams(("parallel", "parallel", "arbitrary")),
        name="sb_p",
    )(q, k, v)


def _merge_kernel(oa_ref, ga_ref, ob_ref, gb_ref, ra_ref, rb_ref, x_ref, wa_ref, wb_ref, wo_ref, gain_ref, y_ref):
    ga = ga_ref[...]
    gb = gb_ref[...]
    a = (oa_ref[...] * (ga * jax.nn.sigmoid(ga))).astype(BF16)
    b = (ob_ref[...] * (gb * jax.nn.sigmoid(gb))).astype(BF16)
    ua = jnp.dot(a, wa_ref[...], preferred_element_type=F32)
    ub = jnp.dot(b, wb_ref[...], preferred_element_type=F32)
    mix = (jax.nn.sigmoid(ra_ref[...]) * ua + jax.nn.sigmoid(rb_ref[...]) * ub).astype(BF16)
    res = x_ref[...] + jnp.dot(mix, wo_ref[...], preferred_element_type=F32)
    ms = jnp.mean(res * res, axis=-1, keepdims=True)
    y_ref[...] = res * lax.rsqrt(ms + RMS_EPS) * gain_ref[...]


def _merge(oa, ga, ob, gb, ra, rb, x2d, wa, wb, wo, gain, tm):
    m, d = x2d.shape
    width = oa.shape[1]
    row = lambda w: pl.BlockSpec((tm, w), lambda i: (i, 0))
    const = lambda s: pl.BlockSpec(s, lambda i: (0, 0), pipeline_mode=pl.Buffered(1))
    return pl.pallas_call(
        _merge_kernel,
        grid=(m // tm,),
        in_specs=[row(width), row(width), row(width), row(width), row(d), row(d), row(d),
                  const((width, d)), const((width, d)), const((d, d)), const((1, d))],
        out_specs=row(d),
        out_shape=jax.ShapeDtypeStruct((m, d), F32),
        compiler_params=_cparams(("parallel",)),
        name="merge",
    )(oa, ga, ob, gb, ra, rb, x2d, wa, wb, wo, gain)


LANES = 128


def _lfaux_kernel(x_ref, sfx_ref, tot_ref, *, nh):
    pb, n = x_ref.shape
    row = lax.broadcasted_iota(jnp.int32, (LANES, LANES), 0)
    col = lax.broadcasted_iota(jnp.int32, (LANES, LANES), 1)
    same_head = (row % nh) == (col % nh)
    m_tot = jnp.where(same_head, 1.0, 0.0).astype(BF16)
    m_sfx = jnp.where(same_head, jnp.where(row > col, 1.0, 0.0), 0.0).astype(BF16)
    carry = jnp.zeros((pb, LANES), F32)
    for a in reversed(range(n // LANES)):
        cols = slice(a * LANES, (a + 1) * LANES)
        x = x_ref[:, cols]
        sfx_ref[:, cols] = _dot_split(x, m_sfx, 3) + carry
        carry = carry + _dot_split(x, m_tot, 3)
    for a in range(n // LANES):
        tot_ref[:, a * LANES:(a + 1) * LANES] = carry


def _logf_page_aux(lf_flat, nh, pb):
    n_phys, n = lf_flat.shape
    blk = pl.BlockSpec((pb, n), lambda i: (i, 0))
    return pl.pallas_call(
        functools.partial(_lfaux_kernel, nh=nh),
        grid=(n_phys // pb,),
        in_specs=[blk],
        out_specs=[blk, blk],
        out_shape=[jax.ShapeDtypeStruct((n_phys, n), F32)] * 2,
        compiler_params=_cparams(("parallel",)),
        name="lfaux",
    )(lf_flat)


def _dec_kernel(pt_ref, qa_ref, kan_ref, van_ref, qb_ref, kbn_ref, vbn_ref, lfn_ref,
                kf_hbm, vf_hbm, ks_hbm, vs_hbm, sfx_hbm, tot_hbm,
                oa_ref, ob_ref,
                kf_buf, vf_buf, ks_buf, vs_buf, sfx_buf, tot_buf, sem,
                qa_s, qb_s, new_pad, m_a, l_a, acc_a, run_f, acc_b, run_s, other_head, sb_pending,
                *, n_pages, page, chunk, scale):
    b = pl.program_id(0)
    c = pl.program_id(1)
    nb = pl.num_programs(0)
    nc = pl.num_programs(1)
    step = b * nc + c
    slot = step % 2
    nt, width = qa_ref.shape
    nh = width // HEAD_DIM
    tp = SUBLANES
    rows = nh * tp
    ck = chunk * page
    prow = page * nh
    fox_caches = ((kf_hbm, kf_buf, 0), (vf_hbm, vf_buf, 1))
    sb_caches = ((ks_hbm, ks_buf, 2), (vs_hbm, vs_buf, 3))

    def page_copies(step_idx, slot_idx, caches, with_aux):
        sb = step_idx // nc
        sc = step_idx % nc
        base = sb * n_pages + (n_pages - (sc + 1) * chunk)
        copies = []
        for i in range(chunk):
            pid = pt_ref[base + i]
            for src, dst, ci in caches:
                copies.append(pltpu.make_async_copy(
                    src.at[pl.ds(pl.multiple_of(pid * prow, prow), prow)],
                    dst.at[slot_idx, pl.ds(i * prow, prow)], sem.at[slot_idx, ci]))
            if with_aux:
                for src, dst in ((sfx_hbm, sfx_buf), (tot_hbm, tot_buf)):
                    copies.append(pltpu.make_async_copy(
                        src.at[pl.ds(pid, 1)], dst.at[slot_idx, pl.ds(i, 1)], sem.at[slot_idx, 4]))
        return copies

    def fox_copies(step_idx, slot_idx):
        return page_copies(step_idx, slot_idx, fox_caches, True)

    def sb_copies(step_idx, slot_idx):
        return page_copies(step_idx, slot_idx, sb_caches, False)

    @pl.when(step == 0)
    def _():
        for cp in fox_copies(step, slot) + sb_copies(step, slot):
            cp.start()
        sb_pending[0] = 1
        new_pad[...] = jnp.zeros(new_pad.shape, F32)
        lane_head = lax.broadcasted_iota(jnp.int32, other_head.shape, 1) % nh
        row_head = lax.broadcasted_iota(jnp.int32, other_head.shape, 0) // tp
        other_head[...] = jnp.where(lane_head == row_head, 0.0, -jnp.inf)

    @pl.when(step + 1 < nb * nc)
    def _():
        for cp in fox_copies(step + 1, 1 - slot):
            cp.start()

    def head(x, h):
        return x[h * tp:(h + 1) * tp]

    def per_head(fn):
        return jnp.concatenate([fn(h) for h in range(nh)], axis=0)

    def spread_heads(x8):
        return per_head(lambda h: jnp.broadcast_to(x8[h:h + 1, :], (tp, x8.shape[1])))

    def scores(q_s, key_fn):
        return per_head(lambda h: lax.dot_general(head(q_s[...], h).astype(BF16), key_fn(h), NT_DIMS,
                                                  preferred_element_type=F32))

    def weighted(p, val_fn):
        pb16 = p.astype(BF16)
        return per_head(lambda h: jnp.dot(head(pb16, h), val_fn(h), preferred_element_type=F32))

    @pl.when(c == 0)
    def _():
        def head_rows(q_ref):
            q = q_ref[...] * scale
            pad = jnp.zeros((tp - nt, HEAD_DIM), F32)
            return jnp.concatenate(
                [piece for h in range(nh) for piece in (q[:, h * HEAD_DIM:(h + 1) * HEAD_DIM], pad)], axis=0)

        qa_s[...] = head_rows(qa_ref)
        qb_s[...] = head_rows(qb_ref)
        tok = lax.broadcasted_iota(jnp.int32, (rows, page), 0) % tp
        key = lax.broadcasted_iota(jnp.int32, (rows, page), 1)
        m01 = _strict_lower_ones(page, BF16)

        def padded(ref, idx):
            new_pad[idx, 0:nt, :] = ref[...]
            return lambda h: new_pad[idx, :, h * HEAD_DIM:(h + 1) * HEAD_DIM].astype(BF16)

        lfn = lfn_ref[...]
        s = scores(qa_s, padded(kan_ref, 0))
        s = jnp.where(key <= tok, s, -jnp.inf) + spread_heads(_dot_split(lfn, m01, 3))
        m_new = jnp.max(s, axis=-1, keepdims=True)
        p = jnp.exp(s - m_new)
        m_a[...] = m_new
        l_a[...] = jnp.sum(p, axis=-1, keepdims=True)
        acc_a[...] = weighted(p, padded(van_ref, 1))
        lane_is_head = (lax.broadcasted_iota(jnp.int32, (nh, prow), 1) % nh
                        == lax.broadcasted_iota(jnp.int32, (nh, prow), 0))
        run_f[...] = jnp.sum(jnp.where(lane_is_head, jnp.sum(lfn, axis=-1, keepdims=True), 0.0),
                             axis=0, keepdims=True)

        z = scores(qb_s, padded(kbn_ref, 2))
        log_keep, log_beta = _sb_scores(z)
        before = key < tok
        log_keep = jnp.where(before, log_keep, 0.0)
        w = jnp.where(before, jnp.exp(log_beta + _dot_split(log_keep, m01, 2)), 0.0)
        acc_b[...] = weighted(w, padded(vbn_ref, 3))
        run_s[...] = jnp.sum(log_keep, axis=-1, keepdims=True)

    for cp in fox_copies(step, slot):
        cp.wait()

    def cached(buf):
        return lambda h: buf[slot, pl.ds(h, ck, stride=nh), :].astype(BF16)

    s = lax.dot_general(qa_s[...].astype(BF16), kf_buf[slot].astype(BF16), NT_DIMS,
                        preferred_element_type=F32)
    run = run_f[...]
    bias = [None] * chunk
    for i in reversed(range(chunk)):
        bias[i] = sfx_buf[slot, i:i + 1, :] + run
        run = run + tot_buf[slot, i:i + 1, :]
    run_f[...] = run
    s = s + jnp.concatenate(bias, axis=1) + jnp.tile(other_head[...], (1, chunk * prow // HEAD_DIM))
    m_old = m_a[...]
    m_new = jnp.maximum(m_old, jnp.max(s, axis=-1, keepdims=True))
    p = jnp.exp(s - m_new)
    alpha = jnp.exp(m_old - m_new)
    m_a[...] = m_new
    l_a[...] = alpha * l_a[...] + jnp.sum(p, axis=-1, keepdims=True)
    acc_a[...] = alpha * acc_a[...] + jnp.dot(p.astype(BF16), vf_buf[slot].astype(BF16),
                                              preferred_element_type=F32)

    sb_live = sb_pending[0] == 1

    @pl.when(sb_live)
    def _():
        for cp in sb_copies(step, slot):
            cp.wait()
        z = scores(qb_s, cached(ks_buf))
        log_keep, log_beta = _sb_scores(z)
        sub = min(ck, SB_CUMSUM_BLOCK)
        m01 = _strict_lower_ones(sub, BF16)
        run = run_s[...]
        w = [None] * (ck // sub)
        for j in reversed(range(ck // sub)):
            cols = slice(j * sub, (j + 1) * sub)
            w[j] = jnp.exp(log_beta[:, cols] + _dot_split(log_keep[:, cols], m01, 2) + run)
            run = run + jnp.sum(log_keep[:, cols], axis=-1, keepdims=True)
        acc_b[...] = acc_b[...] + weighted(jnp.concatenate(w, axis=1), cached(vs_buf))
        run_s[...] = run

    still_alive = jnp.logical_and(sb_live, jnp.max(run_s[...]) > SB_DEAD_LOG)
    fetch_next = jnp.logical_and(step + 1 < nb * nc, jnp.logical_or(c == nc - 1, still_alive))
    sb_pending[0] = fetch_next.astype(jnp.int32)

    @pl.when(fetch_next)
    def _():
        for cp in sb_copies(step + 1, 1 - slot):
            cp.start()

    @pl.when(c == nc - 1)
    def _():
        fa = acc_a[...] / l_a[...]
        fb = acc_b[...]
        for h in range(nh):
            oa_ref[:, h * HEAD_DIM:(h + 1) * HEAD_DIM] = head(fa, h)[0:nt]
            ob_ref[:, h * HEAD_DIM:(h + 1) * HEAD_DIM] = head(fb, h)[0:nt]


def _decode_attention(page_table, qa, kan, van, qb, kbn, vbn, lfn, kf, vf, ks, vs, sfx_flat, tot_flat, chunk):
    nb, nt, width = qa.shape
    nh = width // HEAD_DIM
    n_pages = page_table.shape[1]
    page = kf.shape[2]
    rows = nh * SUBLANES
    as_rows = lambda t: t.reshape(-1, HEAD_DIM)
    nc = n_pages // chunk
    tok = pl.BlockSpec((None, nt, width), lambda b, c, pt: (b, 0, 0))
    hbm = pl.BlockSpec(memory_space=pl.ANY)
    page_buf = pltpu.VMEM((2, chunk * page * nh, HEAD_DIM), F32)
    grid_spec = pltpu.PrefetchScalarGridSpec(
        num_scalar_prefetch=1,
        grid=(nb, nc),
        in_specs=[tok, tok, tok, tok, tok, tok,
                  pl.BlockSpec((None, nh, page), lambda b, c, pt: (b, 0, 0)),
                  hbm, hbm, hbm, hbm, hbm, hbm],
        out_specs=[tok, tok],
        scratch_shapes=[
            page_buf, page_buf, page_buf, page_buf,
            pltpu.VMEM((2, chunk, page * nh), F32),
            pltpu.VMEM((2, chunk, page * nh), F32),
            pltpu.SemaphoreType.DMA((2, 5)),
            pltpu.VMEM((rows, HEAD_DIM), F32),
            pltpu.VMEM((rows, HEAD_DIM), F32),
            pltpu.VMEM((4, page, width), F32),
            pltpu.VMEM((rows, 1), F32),
            pltpu.VMEM((rows, 1), F32),
            pltpu.VMEM((rows, HEAD_DIM), F32),
            pltpu.VMEM((1, page * nh), F32),
            pltpu.VMEM((rows, HEAD_DIM), F32),
            pltpu.VMEM((rows, 1), F32),
            pltpu.VMEM((rows, HEAD_DIM), F32),
            pltpu.SMEM((1,), jnp.int32),
        ],
    )
    return pl.pallas_call(
        functools.partial(_dec_kernel, n_pages=n_pages, page=page, chunk=chunk, scale=HEAD_DIM ** -0.5),
        grid_spec=grid_spec,
        out_shape=[jax.ShapeDtypeStruct((nb, nt, width), F32)] * 2,
        compiler_params=_cparams(("arbitrary", "arbitrary")),
        name="dec",
    )(page_table.reshape(-1), qa, kan, van, qb, kbn, vbn, lfn,
      as_rows(kf), as_rows(vf), as_rows(ks), as_rows(vs), sfx_flat, tot_flat)


def _project(x2d, gain, w_main, w_r, wft, bf, width, tm_norm, tm):
    d = x2d.shape[1]
    h, lft = _norm(x2d, gain, wft, bf, tm_norm)
    groups = [_matmul(h, w_main, g, width, tm, width, "proj_main") for g in range(8)]
    r_a = _matmul(h, w_r, 0, d, tm, width, "proj_gate")
    r_b = _matmul(h, w_r, d // width, d, tm, width, "proj_gate")
    return groups, r_a, r_b, lft


def kernel(x_prompt, x_sample, cache_k_fox, cache_v_fox, cache_logf_fox, cache_k_sb, cache_v_sb, page_table,
           norm_gain, w_in, b_forget, w_branch_fox, w_branch_sb, w_out, final_norm_gain):
    depth = norm_gain.shape[0]
    assert depth == 1, "single-layer step"
    batch, seq, d = x_prompt.shape
    nb, nt, _ = x_sample.shape
    nh = b_forget.shape[1]
    width = nh * HEAD_DIM
    n_phys, page = cache_k_fox.shape[1], cache_k_fox.shape[2]

    w = w_in[0]
    w_main = w[:, :8 * width].astype(BF16)
    wft = w[:, 8 * width:8 * width + nh].T.astype(BF16)
    w_r = w[:, 8 * width + nh:].astype(BF16)
    bf = b_forget[0].reshape(nh, 1)
    gain = norm_gain[0].reshape(1, d)
    fgain = final_norm_gain.reshape(1, d)
    wa = w_branch_fox[0].astype(BF16)
    wb = w_branch_sb[0].astype(BF16)
    wo = w_out[0].astype(BF16)

    xp = x_prompt.reshape(batch * seq, d)
    (qa, ka, va, ga, qb, kb, vb, gb), ra, rb, lft_p = _project(xp, gain, w_main, w_r, wft, bf, width, 256, 512)
    sfx = _suffix_sums(lft_p, seq).reshape(nh, 1, batch * seq)
    oa = _fox_prompt(qa, ka, va, sfx, batch, seq, 512, 2)
    ob = _sb_prompt(qb, kb, vb, batch, seq, 512, 2)
    y_p = _merge(oa, ga, ob, gb, ra, rb, xp, wa, wb, wo, fgain, 256)

    xs = x_sample.reshape(nb * nt, d)
    (qa_s, ka_s, va_s, ga_s, qb_s, kb_s, vb_s, gb_s), ra_s, rb_s, lft_s = _project(
        xs, gain, w_main, w_r, wft, bf, width, nb * nt, nb * nt)
    sfx_flat, tot_flat = _logf_page_aux(cache_logf_fox.reshape(n_phys, page * nh), nh, 256)
    lfn = jnp.pad(jnp.transpose(lft_s.reshape(nh, nb, nt), (1, 0, 2)), ((0, 0), (0, 0), (0, page - nt)))
    tok3 = lambda t: t.reshape(nb, nt, width)
    oa_s, ob_s = _decode_attention(
        page_table, tok3(qa_s), tok3(ka_s), tok3(va_s), tok3(qb_s), tok3(kb_s), tok3(vb_s), lfn,
        cache_k_fox, cache_v_fox, cache_k_sb, cache_v_sb, sfx_flat, tot_flat, 8)
    y_s = _merge(oa_s.reshape(nb * nt, width), ga_s, ob_s.reshape(nb * nt, width), gb_s, ra_s, rb_s, xs,
                 wa, wb, wo, fgain, nb * nt)

    heads_p = lambda t: t.reshape(1, batch, seq, nh, HEAD_DIM)
    heads_s = lambda t: t.reshape(1, nb, nt, nh, HEAD_DIM)
    return (
        y_p.reshape(batch, seq, d), y_s.reshape(nb, nt, d),
        heads_p(ka), heads_p(va), lft_p.T.reshape(1, batch, seq, nh), heads_p(kb), heads_p(vb),
        heads_s(ka_s), heads_s(va_s), lft_s.T.reshape(1, nb, nt, nh), heads_s(kb_s), heads_s(vb_s),
    )
```

```python
import functools

import jax
import jax.numpy as jnp
from jax import lax
from jax.experimental import pallas as pl
from jax.experimental.pallas import tpu as pltpu

HEAD_DIM = 128
RMS_EPS = 1e-6
VMEM_LIMIT_BYTES = 56 * 1024 * 1024
SUBLANES = 8
F32 = jnp.float32
BF16 = jnp.bfloat16
NT_DIMS = (((1,), (1,)), ((), ()))


def _cparams(sem):
    return pltpu.CompilerParams(dimension_semantics=sem, vmem_limit_bytes=VMEM_LIMIT_BYTES)


def _log_sigmoid(u):
    return jnp.minimum(u, 0.0) - jnp.log1p(jnp.exp(-jnp.abs(u)))


def _strict_lower_ones(n, dtype):
    row = lax.broadcasted_iota(jnp.int32, (n, n), 0)
    col = lax.broadcasted_iota(jnp.int32, (n, n), 1)
    return jnp.where(row > col, 1.0, 0.0).astype(dtype)


def _dot_split(x, m01, terms):
    acc = None
    rem = x
    for i in range(terms):
        piece = rem.astype(BF16)
        part = jnp.dot(piece, m01, preferred_element_type=F32)
        acc = part if acc is None else acc + part
        if i + 1 < terms:
            rem = rem - piece.astype(F32)
    return acc


def _norm_kernel(x_ref, gain_ref, wft_ref, bf_ref, h_ref, lft_ref):
    x = x_ref[...]
    ms = jnp.mean(x * x, axis=-1, keepdims=True)
    hb = (x * lax.rsqrt(ms + RMS_EPS) * gain_ref[...]).astype(BF16)
    h_ref[...] = hb
    f = lax.dot_general(wft_ref[...], hb, NT_DIMS, preferred_element_type=F32)
    lft_ref[...] = _log_sigmoid(f + bf_ref[...])


def _norm(x2d, gain, wft, bf, tm):
    m, d = x2d.shape
    nh = wft.shape[0]
    return pl.pallas_call(
        _norm_kernel,
        grid=(m // tm,),
        in_specs=[
            pl.BlockSpec((tm, d), lambda i: (i, 0)),
            pl.BlockSpec((1, d), lambda i: (0, 0)),
            pl.BlockSpec((nh, d), lambda i: (0, 0)),
            pl.BlockSpec((nh, 1), lambda i: (0, 0)),
        ],
        out_specs=[
            pl.BlockSpec((tm, d), lambda i: (i, 0)),
            pl.BlockSpec((nh, tm), lambda i: (0, i)),
        ],
        out_shape=[
            jax.ShapeDtypeStruct((m, d), BF16),
            jax.ShapeDtypeStruct((nh, m), F32),
        ],
        compiler_params=_cparams(("parallel",)),
        name="norm",
    )(x2d, gain, wft, bf)


def _mm_kernel(h_ref, w_ref, o_ref):
    o_ref[...] = jnp.dot(h_ref[...], w_ref[...], preferred_element_type=F32)


def _matmul(h, w, col_block0, n, tm, tn, name):
    m, k = h.shape
    return pl.pallas_call(
        _mm_kernel,
        grid=(m // tm, n // tn),
        in_specs=[
            pl.BlockSpec((tm, k), lambda i, j: (i, 0)),
            pl.BlockSpec((k, tn), lambda i, j: (0, col_block0 + j)),
        ],
        out_specs=pl.BlockSpec((tm, tn), lambda i, j: (i, j)),
        out_shape=jax.ShapeDtypeStruct((m, n), F32),
        compiler_params=_cparams(("parallel", "parallel")),
        name=name,
    )(h, w)


def _sfx_kernel(lf_ref, o_ref, *, blk):
    t = lf_ref.shape[1]
    m01 = _strict_lower_ones(blk, BF16)
    carry = jnp.zeros((lf_ref.shape[0], 1), F32)
    for c in reversed(range(t // blk)):
        x = lf_ref[:, c * blk:(c + 1) * blk]
        o_ref[:, c * blk:(c + 1) * blk] = _dot_split(x, m01, 3) + carry
        carry = carry + jnp.sum(x, axis=-1, keepdims=True)


def _suffix_sums(lft, seq):
    nh, m = lft.shape
    return pl.pallas_call(
        functools.partial(_sfx_kernel, blk=256),
        grid=(m // seq,),
        in_specs=[pl.BlockSpec((nh, seq), lambda b: (0, b))],
        out_specs=pl.BlockSpec((nh, seq), lambda b: (0, b)),
        out_shape=jax.ShapeDtypeStruct((nh, m), F32),
        compiler_params=_cparams(("parallel",)),
        name="sfx",
    )(lft)


def _head_cols(g):
    return slice(g * HEAD_DIM, (g + 1) * HEAD_DIM)


def _fox_kernel(q_ref, k_ref, v_ref, sfx_ref, o_ref, *, tq, heads, scale):
    qi = pl.program_id(2)
    qs = [(q_ref[:, _head_cols(g)] * scale).astype(BF16) for g in range(heads)]

    def step(kb, carry, diagonal):
        off = pl.multiple_of(kb * tq, tq)
        if diagonal:
            row = lax.broadcasted_iota(jnp.int32, (tq, tq), 0)
            col = lax.broadcasted_iota(jnp.int32, (tq, tq), 1)
            visible = col <= row
        out = []
        for g in range(heads):
            m_run, l_run, acc = carry[g]
            k = k_ref[pl.ds(off, tq), _head_cols(g)].astype(BF16)
            v = v_ref[pl.ds(off, tq), _head_cols(g)].astype(BF16)
            s = lax.dot_general(qs[g], k, NT_DIMS, preferred_element_type=F32)
            s = s + sfx_ref[g, :, pl.ds(off, tq)]
            if diagonal:
                s = jnp.where(visible, s, -jnp.inf)
            m_new = jnp.maximum(m_run, jnp.max(s, axis=-1, keepdims=True))
            p = jnp.exp(s - m_new)
            alpha = jnp.exp(m_run - m_new)
            l_new = alpha * l_run + jnp.sum(p, axis=-1, keepdims=True)
            acc = alpha * acc + jnp.dot(p.astype(BF16), v, preferred_element_type=F32)
            out.append((m_new, l_new, acc))
        return tuple(out)

    init = tuple((jnp.full((tq, 1), -jnp.inf, F32), jnp.zeros((tq, 1), F32), jnp.zeros((tq, HEAD_DIM), F32))
                 for _ in range(heads))
    carry = lax.fori_loop(0, qi, lambda kb, c: step(kb, c, False), init)
    carry = step(qi, carry, True)
    for g in range(heads):
        o_ref[:, _head_cols(g)] = carry[g][2] / carry[g][1]


def _fox_prompt(q, k, v, sfx3, batch, seq, tq, heads):
    m, width = q.shape
    nh = width // HEAD_DIM
    nq = seq // tq
    cols = heads * HEAD_DIM
    return pl.pallas_call(
        functools.partial(_fox_kernel, tq=tq, heads=heads, scale=HEAD_DIM ** -0.5),
        grid=(batch, nh // heads, nq),
        in_specs=[
            pl.BlockSpec((tq, cols), lambda b, h, i: (b * nq + i, h)),
            pl.BlockSpec((seq, cols), lambda b, h, i: (b, h)),
            pl.BlockSpec((seq, cols), lambda b, h, i: (b, h)),
            pl.BlockSpec((heads, 1, seq), lambda b, h, i: (h, 0, b)),
        ],
        out_specs=pl.BlockSpec((tq, cols), lambda b, h, i: (b * nq + i, h)),
        out_shape=jax.ShapeDtypeStruct((m, width), F32),
        compiler_params=_cparams(("parallel", "parallel", "arbitrary")),
        name="fox_p",
    )(q, k, v, sfx3)


def _sb_scores(z):
    l1 = jnp.log1p(jnp.exp(-jnp.abs(z)))
    return -(jnp.maximum(z, 0.0) + l1), jnp.minimum(z, 0.0) - l1


SB_DEAD_LOG = -120.0
SB_CUMSUM_BLOCK = 512


def _sb_kernel(q_ref, k_ref, v_ref, o_ref, *, tq, tk, heads, scale):
    qi = pl.program_id(2)
    qs = [(q_ref[:, _head_cols(g)] * scale).astype(BF16) for g in range(heads)]

    def step(off, width, carry, diagonal):
        m01 = _strict_lower_ones(width, BF16)
        if diagonal:
            row = lax.broadcasted_iota(jnp.int32, (tq, width), 0)
            col = lax.broadcasted_iota(jnp.int32, (tq, width), 1)
            before = col < row
        out = []
        for g in range(heads):
            run, acc = carry[g]
            k = k_ref[pl.ds(off, width), _head_cols(g)].astype(BF16)
            v = v_ref[pl.ds(off, width), _head_cols(g)].astype(BF16)
            z = lax.dot_general(qs[g], k, NT_DIMS, preferred_element_type=F32)
            log_keep, log_beta = _sb_scores(z)
            if diagonal:
                log_keep = jnp.where(before, log_keep, 0.0)
            later = _dot_split(log_keep, m01, 2) + run
            w = jnp.exp(log_beta + later)
            if diagonal:
                w = jnp.where(before, w, 0.0)
            acc = acc + jnp.dot(w.astype(BF16), v, preferred_element_type=F32)
            run = run + jnp.sum(log_keep, axis=-1, keepdims=True)
            out.append((run, acc))
        return tuple(out)

    def alive(carry):
        top = carry[0][0]
        for g in range(1, heads):
            top = jnp.maximum(top, carry[g][0])
        return jnp.max(top) > SB_DEAD_LOG

    init = tuple((jnp.zeros((tq, 1), F32), jnp.zeros((tq, HEAD_DIM), F32)) for _ in range(heads))
    carry = step(pl.multiple_of(qi * tq, tq), tq, init, True)
    n_left = qi * (tq // tk)

    def body(state):
        i, _, carry = state
        carry = step(pl.multiple_of(qi * tq - (i + 1) * tk, tk), tk, carry, False)
        return i + 1, alive(carry), carry

    _, _, carry = lax.while_loop(lambda st: jnp.logical_and(st[0] < n_left, st[1]), body,
                                 (jnp.int32(0), alive(carry), carry))
    for g in range(heads):
        o_ref[:, _head_cols(g)] = carry[g][1]


def _sb_prompt(q, k, v, batch, seq, tq, heads):
    m, width = q.shape
    nh = width // HEAD_DIM
    nq = seq // tq
    cols = heads * HEAD_DIM
    return pl.pallas_call(
        functools.partial(_sb_kernel, tq=tq, tk=256, heads=heads, scale=HEAD_DIM ** -0.5),
        grid=(batch, nh // heads, nq),
        in_specs=[
            pl.BlockSpec((tq, cols), lambda b, h, i: (b * nq + i, h)),
            pl.BlockSpec((seq, cols), lambda b, h, i: (b, h)),
            pl.BlockSpec((seq, cols), lambda b, h, i: (b, h)),
        ],
        out_specs=pl.BlockSpec((tq, cols), lambda b, h, i: (b * nq + i, h)),
        out_shape=jax.ShapeDtypeStruct((m, width), F32),
        compiler_params=_cparams(("parallel", "parallel", "arbitrary")),
        name="sb_p",
    )(q, k, v)


def _merge_kernel(oa_ref, ga_ref, ob_ref, gb_ref, ra_ref, rb_ref, x_ref, wa_ref, wb_ref, wo_ref, gain_ref, y_ref):
    ga = ga_ref[...]
    gb = gb_ref[...]
    a = (oa_ref[...] * (ga * jax.nn.sigmoid(ga))).astype(BF16)
    b = (ob_ref[...] * (gb * jax.nn.sigmoid(gb))).astype(BF16)
    ua = jnp.dot(a, wa_ref[...], preferred_element_type=F32)
    ub = jnp.dot(b, wb_ref[...], preferred_element_type=F32)
    mix = (jax.nn.sigmoid(ra_ref[...]) * ua + jax.nn.sigmoid(rb_ref[...]) * ub).astype(BF16)
    res = x_ref[...] + jnp.dot(mix, wo_ref[...], preferred_element_type=F32)
    ms = jnp.mean(res * res, axis=-1, keepdims=True)
    y_ref[...] = res * lax.rsqrt(ms + RMS_EPS) * gain_ref[...]


def _merge(oa, ga, ob, gb, ra, rb, x2d, wa, wb, wo, gain, tm):
    m, d = x2d.shape
    width = oa.shape[1]
    row = lambda w: pl.BlockSpec((tm, w), lambda i: (i, 0))
    const = lambda s: pl.BlockSpec(s, lambda i: (0, 0), pipeline_mode=pl.Buffered(1))
    return pl.pallas_call(
        _merge_kernel,
        grid=(m // tm,),
        in_specs=[row(width), row(width), row(width), row(width), row(d), row(d), row(d),
                  const((width, d)), const((width, d)), const((d, d)), const((1, d))],
        out_specs=row(d),
        out_shape=jax.ShapeDtypeStruct((m, d), F32),
        compiler_params=_cparams(("parallel",)),
        name="merge",
    )(oa, ga, ob, gb, ra, rb, x2d, wa, wb, wo, gain)


LANES = 128


def _lfaux_kernel(x_ref, sfx_ref, tot_ref, *, nh):
    pb, n = x_ref.shape
    row = lax.broadcasted_iota(jnp.int32, (LANES, LANES), 0)
    col = lax.broadcasted_iota(jnp.int32, (LANES, LANES), 1)
    same_head = (row % nh) == (col % nh)
    m_tot = jnp.where(same_head, 1.0, 0.0).astype(BF16)
    m_sfx = jnp.where(same_head, jnp.where(row > col, 1.0, 0.0), 0.0).astype(BF16)
    carry = jnp.zeros((pb, LANES), F32)
    for a in reversed(range(n // LANES)):
        cols = slice(a * LANES, (a + 1) * LANES)
        x = x_ref[:, cols]
        sfx_ref[:, cols] = _dot_split(x, m_sfx, 3) + carry
        carry = carry + _dot_split(x, m_tot, 3)
    for a in range(n // LANES):
        tot_ref[:, a * LANES:(a + 1) * LANES] = carry


def _logf_page_aux(lf_flat, nh, pb):
    n_phys, n = lf_flat.shape
    blk = pl.BlockSpec((pb, n), lambda i: (i, 0))
    return pl.pallas_call(
        functools.partial(_lfaux_kernel, nh=nh),
        grid=(n_phys // pb,),
        in_specs=[blk],
        out_specs=[blk, blk],
        out_shape=[jax.ShapeDtypeStruct((n_phys, n), F32)] * 2,
        compiler_params=_cparams(("parallel",)),
        name="lfaux",
    )(lf_flat)


def _dec_kernel(pt_ref, qa_ref, kan_ref, van_ref, qb_ref, kbn_ref, vbn_ref, lfn_ref,
                kf_hbm, vf_hbm, ks_hbm, vs_hbm, sfx_hbm, tot_hbm,
                oa_ref, ob_ref,
                kf_buf, vf_buf, ks_buf, vs_buf, sfx_buf, tot_buf, sem,
                qa_s, qb_s, new_pad, m_a, l_a, acc_a, run_f, acc_b, run_s, other_head, sb_pending,
                *, n_pages, page, chunk, scale):
    b = pl.program_id(0)
    c = pl.program_id(1)
    nb = pl.num_programs(0)
    nc = pl.num_programs(1)
    step = b * nc + c
    slot = step % 2
    nt, width = qa_ref.shape
    nh = width // HEAD_DIM
    tp = SUBLANES
    rows = nh * tp
    ck = chunk * page
    prow = page * nh
    fox_caches = ((kf_hbm, kf_buf, 0), (vf_hbm, vf_buf, 1))
    sb_caches = ((ks_hbm, ks_buf, 2), (vs_hbm, vs_buf, 3))

    def page_copies(step_idx, slot_idx, caches, with_aux):
        sb = step_idx // nc
        sc = step_idx % nc
        base = sb * n_pages + (n_pages - (sc + 1) * chunk)
        copies = []
        for i in range(chunk):
            pid = pt_ref[base + i]
            for src, dst, ci in caches:
                copies.append(pltpu.make_async_copy(
                    src.at[pl.ds(pl.multiple_of(pid * prow, prow), prow)],
                    dst.at[slot_idx, pl.ds(i * prow, prow)], sem.at[slot_idx, ci]))
            if with_aux:
                for src, dst in ((sfx_hbm, sfx_buf), (tot_hbm, tot_buf)):
                    copies.append(pltpu.make_async_copy(
                        src.at[pl.ds(pid, 1)], dst.at[slot_idx, pl.ds(i, 1)], sem.at[slot_idx, 4]))
        return copies

    def fox_copies(step_idx, slot_idx):
        return page_copies(step_idx, slot_idx, fox_caches, True)

    def sb_copies(step_idx, slot_idx):
        return page_copies(step_idx, slot_idx, sb_caches, False)

    @pl.when(step == 0)
    def _():
        for cp in fox_copies(step, slot) + sb_copies(step, slot):
            cp.start()
        sb_pending[0] = 1
        new_pad[...] = jnp.zeros(new_pad.shape, F32)
        lane_head = lax.broadcasted_iota(jnp.int32, other_head.shape, 1) % nh
        row_head = lax.broadcasted_iota(jnp.int32, other_head.shape, 0) // tp
        other_head[...] = jnp.where(lane_head == row_head, 0.0, -jnp.inf)

    @pl.when(step + 1 < nb * nc)
    def _():
        for cp in fox_copies(step + 1, 1 - slot):
            cp.start()

    def head(x, h):
        return x[h * tp:(h + 1) * tp]

    def per_head(fn):
        return jnp.concatenate([fn(h) for h in range(nh)], axis=0)

    def spread_heads(x8):
        return per_head(lambda h: jnp.broadcast_to(x8[h:h + 1, :], (tp, x8.shape[1])))

    def scores(q_s, key_fn):
        return per_head(lambda h: lax.dot_general(head(q_s[...], h).astype(BF16), key_fn(h), NT_DIMS,
                                                  preferred_element_type=F32))

    def weighted(p, val_fn):
        pb16 = p.astype(BF16)
        return per_head(lambda h: jnp.dot(head(pb16, h), val_fn(h), preferred_element_type=F32))

    @pl.when(c == 0)
    def _():
        def head_rows(q_ref):
            q = q_ref[...] * scale
            pad = jnp.zeros((tp - nt, HEAD_DIM), F32)
            return jnp.concatenate(
                [piece for h in range(nh) for piece in (q[:, h * HEAD_DIM:(h + 1) * HEAD_DIM], pad)], axis=0)

        qa_s[...] = head_rows(qa_ref)
        qb_s[...] = head_rows(qb_ref)
        tok = lax.broadcasted_iota(jnp.int32, (rows, page), 0) % tp
        key = lax.broadcasted_iota(jnp.int32, (rows, page), 1)
        m01 = _strict_lower_ones(page, BF16)

        def padded(ref, idx):
            new_pad[idx, 0:nt, :] = ref[...]
            return lambda h: new_pad[idx, :, h * HEAD_DIM:(h + 1) * HEAD_DIM].astype(BF16)

        lfn = lfn_ref[...]
        s = scores(qa_s, padded(kan_ref, 0))
        s = jnp.where(key <= tok, s, -jnp.inf) + spread_heads(_dot_split(lfn, m01, 3))
        m_new = jnp.max(s, axis=-1, keepdims=True)
        p = jnp.exp(s - m_new)
        m_a[...] = m_new
        l_a[...] = jnp.sum(p, axis=-1, keepdims=True)
        acc_a[...] = weighted(p, padded(van_ref, 1))
        lane_is_head = (lax.broadcasted_iota(jnp.int32, (nh, prow), 1) % nh
                        == lax.broadcasted_iota(jnp.int32, (nh, prow), 0))
        run_f[...] = jnp.sum(jnp.where(lane_is_head, jnp.sum(lfn, axis=-1, keepdims=True), 0.0),
                             axis=0, keepdims=True)

        z = scores(qb_s, padded(kbn_ref, 2))
        log_keep, log_beta = _sb_scores(z)
        before = key < tok
        log_keep = jnp.where(before, log_keep, 0.0)
        w = jnp.where(before, jnp.exp(log_beta + _dot_split(log_keep, m01, 2)), 0.0)
        acc_b[...] = weighted(w, padded(vbn_ref, 3))
        run_s[...] = jnp.sum(log_keep, axis=-1, keepdims=True)

    for cp in fox_copies(step, slot):
        cp.wait()

    def cached(buf):
        return lambda h: buf[slot, pl.ds(h, ck, stride=nh), :].astype(BF16)

    s = lax.dot_general(qa_s[...].astype(BF16), kf_buf[slot].astype(BF16), NT_DIMS,
                        preferred_element_type=F32)
    run = run_f[...]
    bias = [None] * chunk
    for i in reversed(range(chunk)):
        bias[i] = sfx_buf[slot, i:i + 1, :] + run
        run = run + tot_buf[slot, i:i + 1, :]
    run_f[...] = run
    s = s + jnp.concatenate(bias, axis=1) + jnp.tile(other_head[...], (1, chunk * prow // HEAD_DIM))
    m_old = m_a[...]
    m_new = jnp.maximum(m_old, jnp.max(s, axis=-1, keepdims=True))
    p = jnp.exp(s - m_new)
    alpha = jnp.exp(m_old - m_new)
    m_a[...] = m_new
    l_a[...] = alpha * l_a[...] + jnp.sum(p, axis=-1, keepdims=True)
    acc_a[...] = alpha * acc_a[...] + jnp.dot(p.astype(BF16), vf_buf[slot].astype(BF16),
                                              preferred_element_type=F32)

    sb_live = sb_pending[0] == 1

    @pl.when(sb_live)
    def _():
        for cp in sb_copies(step, slot):
            cp.wait()
        z = scores(qb_s, cached(ks_buf))
        log_keep, log_beta = _sb_scores(z)
        sub = min(ck, SB_CUMSUM_BLOCK)
        m01 = _strict_lower_ones(sub, BF16)
        run = run_s[...]
        w = [None] * (ck // sub)
        for j in reversed(range(ck // sub)):
            cols = slice(j * sub, (j + 1) * sub)
            w[j] = jnp.exp(log_beta[:, cols] + _dot_split(log_keep[:, cols], m01, 2) + run)
            run = run + jnp.sum(log_keep[:, cols], axis=-1, keepdims=True)
        acc_b[...] = acc_b[...] + weighted(jnp.concatenate(w, axis=1), cached(vs_buf))
        run_s[...] = run

    still_alive = jnp.logical_and(sb_live, jnp.max(run_s[...]) > SB_DEAD_LOG)
    fetch_next = jnp.logical_and(step + 1 < nb * nc, jnp.logical_or(c == nc - 1, still_alive))
    sb_pending[0] = fetch_next.astype(jnp.int32)

    @pl.when(fetch_next)
    def _():
        for cp in sb_copies(step + 1, 1 - slot):
            cp.start()

    @pl.when(c == nc - 1)
    def _():
        fa = acc_a[...] / l_a[...]
        fb = acc_b[...]
        for h in range(nh):
            oa_ref[:, h * HEAD_DIM:(h + 1) * HEAD_DIM] = head(fa, h)[0:nt]
            ob_ref[:, h * HEAD_DIM:(h + 1) * HEAD_DIM] = head(fb, h)[0:nt]


def _decode_attention(page_table, qa, kan, van, qb, kbn, vbn, lfn, kf, vf, ks, vs, sfx_flat, tot_flat, chunk):
    nb, nt, width = qa.shape
    nh = width // HEAD_DIM
    n_pages = page_table.shape[1]
    page = kf.shape[2]
    rows = nh * SUBLANES
    as_rows = lambda t: t.reshape(-1, HEAD_DIM)
    nc = n_pages // chunk
    tok = pl.BlockSpec((None, nt, width), lambda b, c, pt: (b, 0, 0))
    hbm = pl.BlockSpec(memory_space=pl.ANY)
    page_buf = pltpu.VMEM((2, chunk * page * nh, HEAD_DIM), F32)
    grid_spec = pltpu.PrefetchScalarGridSpec(
        num_scalar_prefetch=1,
        grid=(nb, nc),
        in_specs=[tok, tok, tok, tok, tok, tok,
                  pl.BlockSpec((None, nh, page), lambda b, c, pt: (b, 0, 0)),
                  hbm, hbm, hbm, hbm, hbm, hbm],
        out_specs=[tok, tok],
        scratch_shapes=[
            page_buf, page_buf, page_buf, page_buf,
            pltpu.VMEM((2, chunk, page * nh), F32),
            pltpu.VMEM((2, chunk, page * nh), F32),
            pltpu.SemaphoreType.DMA((2, 5)),
            pltpu.VMEM((rows, HEAD_DIM), F32),
            pltpu.VMEM((rows, HEAD_DIM), F32),
            pltpu.VMEM((4, page, width), F32),
            pltpu.VMEM((rows, 1), F32),
            pltpu.VMEM((rows, 1), F32),
            pltpu.VMEM((rows, HEAD_DIM), F32),
            pltpu.VMEM((1, page * nh), F32),
            pltpu.VMEM((rows, HEAD_DIM), F32),
            pltpu.VMEM((rows, 1), F32),
            pltpu.VMEM((rows, HEAD_DIM), F32),
            pltpu.SMEM((1,), jnp.int32),
        ],
    )
    return pl.pallas_call(
        functools.partial(_dec_kernel, n_pages=n_pages, page=page, chunk=chunk, scale=HEAD_DIM ** -0.5),
        grid_spec=grid_spec,
        out_shape=[jax.ShapeDtypeStruct((nb, nt, width), F32)] * 2,
        compiler_params=_cparams(("arbitrary", "arbitrary")),
        name="dec",
    )(page_table.reshape(-1), qa, kan, van, qb, kbn, vbn, lfn,
      as_rows(kf), as_rows(vf), as_rows(ks), as_rows(vs), sfx_flat, tot_flat)


def _project(x2d, gain, w_main, w_r, wft, bf, width, tm_norm, tm):
    d = x2d.shape[1]
    h, lft = _norm(x2d, gain, wft, bf, tm_norm)
    groups = [_matmul(h, w_main, g, width, tm, width, "proj_main") for g in range(8)]
    r_a = _matmul(h, w_r, 0, d, tm, width, "proj_gate")
    r_b = _matmul(h, w_r, d // width, d, tm, width, "proj_gate")
    return groups, r_a, r_b, lft


def kernel(x_prompt, x_sample, cache_k_fox, cache_v_fox, cache_logf_fox, cache_k_sb, cache_v_sb, page_table,
           norm_gain, w_in, b_forget, w_branch_fox, w_branch_sb, w_out, final_norm_gain):
    depth = norm_gain.shape[0]
    assert depth == 1, "single-layer step"
    batch, seq, d = x_prompt.shape
    nb, nt, _ = x_sample.shape
    nh = b_forget.shape[1]
    width = nh * HEAD_DIM
    n_phys, page = cache_k_fox.shape[1], cache_k_fox.shape[2]

    w = w_in[0]
    w_main = w[:, :8 * width].astype(BF16)
    wft = w[:, 8 * width:8 * width + nh].T.astype(BF16)
    w_r = w[:, 8 * width + nh:].astype(BF16)
    bf = b_forget[0].reshape(nh, 1)
    gain = norm_gain[0].reshape(1, d)
    fgain = final_norm_gain.reshape(1, d)
    wa = w_branch_fox[0].astype(BF16)
    wb = w_branch_sb[0].astype(BF16)
    wo = w_out[0].astype(BF16)

    xp = x_prompt.reshape(batch * seq, d)
    (qa, ka, va, ga, qb, kb, vb, gb), ra, rb, lft_p = _project(xp, gain, w_main, w_r, wft, bf, width, 256, 512)
    sfx = _suffix_sums(lft_p, seq).reshape(nh, 1, batch * seq)
    oa = _fox_prompt(qa, ka, va, sfx, batch, seq, 512, 2)
    ob = _sb_prompt(qb, kb, vb, batch, seq, 512, 2)
    y_p = _merge(oa, ga, ob, gb, ra, rb, xp, wa, wb, wo, fgain, 256)

    xs = x_sample.reshape(nb * nt, d)
    (qa_s, ka_s, va_s, ga_s, qb_s, kb_s, vb_s, gb_s), ra_s, rb_s, lft_s = _project(
        xs, gain, w_main, w_r, wft, bf, width, nb * nt, nb * nt)
    sfx_flat, tot_flat = _logf_page_aux(cache_logf_fox.reshape(n_phys, page * nh), nh, 256)
    lfn = jnp.pad(jnp.transpose(lft_s.reshape(nh, nb, nt), (1, 0, 2)), ((0, 0), (0, 0), (0, page - nt)))
    tok3 = lambda t: t.reshape(nb, nt, width)
    oa_s, ob_s = _decode_attention(
        page_table, tok3(qa_s), tok3(ka_s), tok3(va_s), tok3(qb_s), tok3(kb_s), tok3(vb_s), lfn,
        cache_k_fox, cache_v_fox, cache_k_sb, cache_v_sb, sfx_flat, tot_flat, 8)
    y_s = _merge(oa_s.reshape(nb * nt, width), ga_s, ob_s.reshape(nb * nt, width), gb_s, ra_s, rb_s, xs,
                 wa, wb, wo, fgain, nb * nt)

    heads_p = lambda t: t.reshape(1, batch, seq, nh, HEAD_DIM)
    heads_s = lambda t: t.reshape(1, nb, nt, nh, HEAD_DIM)
    return (
        y_p.reshape(batch, seq, d), y_s.reshape(nb, nt, d),
        heads_p(ka), heads_p(va), lft_p.T.reshape(1, batch, seq, nh), heads_p(kb), heads_p(vb),
        heads_s(ka_s), heads_s(va_s), lft_s.T.reshape(1, nb, nt, nh), heads_s(kb_s), heads_s(vb_s),
    )
```
